```python
import jax, jax.numpy as jnp
from jax import lax
import numpy as np

D_MODEL = 1024
BATCH = 8
SEQ = 2048
DEPTH = 2
DEC_BATCH = 128
DEC_SEQ = 8
PAST_LEN = 16384
PAGE_SIZE = 128

N_META = 16
D_CONV = D_MODEL
CONV_W = 31
D_POOL = D_MODEL
POOL_WINDOWS = (2, 4, 8, 16)
N_POOL_GROUPS = len(POOL_WINDOWS)
POOL_GC = D_POOL // N_POOL_GROUPS
POOL_MAX = max(POOL_WINDOWS)
D_FF = ((8 * D_MODEL // 3 + 127) // 128) * 128
FFN_CONV_W = 3
D_IN = 2 * D_CONV + D_POOL + 2 * D_MODEL
EPS = 1e-6

kernel_name = "hybrid_conformer_pool_convffn_decoder_step"


def rmsnorm(x, g):
    xf = x.astype(jnp.float32)
    y = xf * lax.rsqrt(jnp.mean(xf * xf, axis=-1, keepdims=True) + EPS)
    return (y * g.astype(jnp.float32)).astype(x.dtype)


def layernorm(x, g, b):
    xf = x.astype(jnp.float32)
    mu = jnp.mean(xf, axis=-1, keepdims=True)
    xc = xf - mu
    var = jnp.mean(xc * xc, axis=-1, keepdims=True)
    y = xc * lax.rsqrt(var + EPS) * g.astype(jnp.float32) + b.astype(jnp.float32)
    return y.astype(x.dtype)


def causal_dwconv(u, prev, w):
    k = w.shape[0]
    c = u.shape[-1]
    ext = jnp.concatenate([prev.astype(u.dtype), u], axis=1)
    y = lax.conv_general_dilated(
        ext, w.astype(ext.dtype)[:, None, :], window_strides=(1,), padding="VALID",
        dimension_numbers=("NWC", "WIO", "NWC"), feature_group_count=c)
    return y, ext[:, ext.shape[1] - (k - 1):]


def multiscale_pool(u, prev, pos0):
    b, t, _ = u.shape
    lp = POOL_MAX - 1
    ext = jnp.concatenate([prev.astype(u.dtype), u], axis=1)
    extf = ext.astype(jnp.float32)
    csum = jnp.concatenate([jnp.zeros((b, 1, extf.shape[-1]), jnp.float32),
                            lax.cumsum(extf, axis=1)], axis=1)
    pos = pos0 + jnp.arange(t, dtype=jnp.int32)
    means = []
    for gi, win in enumerate(POOL_WINDOWS):
        sl = slice(gi * POOL_GC, (gi + 1) * POOL_GC)
        s = csum[:, lp + 1:lp + 1 + t, sl] - csum[:, lp + 1 - win:lp + 1 - win + t, sl]
        cnt = jnp.minimum(pos + 1, win).astype(jnp.float32)
        means.append(s / cnt[None, :, None])
    mean = jnp.concatenate(means, axis=-1)
    y = (mean - u.astype(jnp.float32)).astype(u.dtype)
    return y, ext[:, ext.shape[1] - lp:]


def layer(x, prev_conv, prev_pool, prev_ffn, pos0, norm1_g, w_in, conv_dw, conv_b,
          ln_g, ln_b, w_conv_out, w_pool, pool_scale, w_out, norm2_g, w_up, ffn_dw, w_down):
    b, t, _ = x.shape
    h = rmsnorm(x, norm1_g)
    z = h @ w_in
    o1, o2, o3, o4 = D_CONV, 2 * D_CONV, 2 * D_CONV + D_POOL, 2 * D_CONV + D_POOL + D_MODEL
    za, zg, zp = z[..., :o1], z[..., o1:o2], z[..., o2:o3]
    gate_a, gate_b = z[..., o3:o4], z[..., o4:]
    a = za * jax.nn.sigmoid(zg)
    a, new_conv = causal_dwconv(a, prev_conv, conv_dw)
    a = layernorm(a + conv_b, ln_g, ln_b)
    a = jax.nn.silu(a) @ w_conv_out
    p, new_pool = multiscale_pool(zp, prev_pool, pos0)
    p = jnp.einsum("btgc,gcd->btgd", p.reshape(b, t, N_POOL_GROUPS, POOL_GC), w_pool)
    p = p.reshape(b, t, D_POOL) * pool_scale
    m = jax.nn.sigmoid(gate_a) * a + jax.nn.sigmoid(gate_b) * p
    x = x + m @ w_out
    h2 = rmsnorm(x, norm2_g)
    up = h2 @ w_up
    up, new_ffn = causal_dwconv(up, prev_ffn, ffn_dw)
    v, g = up[..., :D_FF], up[..., D_FF:]
    x = x + (jax.nn.gelu(g) * v) @ w_down
    return x, new_conv, new_pool, new_ffn


def run_trunk(x, st_conv, st_pool, st_ffn, pos0, norm1_g, w_in, conv_dw, conv_b, ln_g, ln_b,
              w_conv_out, w_pool, pool_scale, w_out, norm2_g, w_up, ffn_dw, w_down, final_g):
    ncs, nps, nfs = [], [], []
    for l in range(DEPTH):
        x, nc, np_, nf = layer(x, st_conv[l], st_pool[l], st_ffn[l], pos0,
                               norm1_g[l], w_in[l], conv_dw[l], conv_b[l], ln_g[l], ln_b[l],
                               w_conv_out[l], w_pool[l], pool_scale[l], w_out[l],
                               norm2_g[l], w_up[l], ffn_dw[l], w_down[l])
        ncs.append(nc); nps.append(np_); nfs.append(nf)
    return rmsnorm(x, final_g), jnp.stack(ncs), jnp.stack(nps), jnp.stack(nfs)


def setup_inputs(seed: int = 0) -> dict:
    key = jax.random.key(seed)
    ks = jax.random.split(key, 24)
    f32 = jnp.float32
    nrm = lambda k, shape, s: (jax.random.normal(k, shape, f32) * s)
    return {
        "x_prompt": nrm(ks[0], (BATCH, SEQ, D_MODEL), 1.0),
        "x_sample": nrm(ks[1], (DEC_BATCH, DEC_SEQ, D_MODEL), 1.0),
        "state_conv": nrm(ks[2], (DEPTH, DEC_BATCH, CONV_W - 1, D_CONV), 0.5),
        "state_pool": nrm(ks[3], (DEPTH, DEC_BATCH, POOL_MAX - 1, D_POOL), 1.0),
        "state_ffn": nrm(ks[4], (DEPTH, DEC_BATCH, FFN_CONV_W - 1, 2 * D_FF), 1.0),
        "meta_tokens": nrm(ks[5], (N_META, D_MODEL), 1.0),
        "norm1_g": 1.0 + nrm(ks[6], (DEPTH, D_MODEL), 0.02),
        "w_in": nrm(ks[7], (DEPTH, D_MODEL, D_IN), D_MODEL ** -0.5),
        "conv_dw": nrm(ks[8], (DEPTH, CONV_W, D_CONV), CONV_W ** -0.5),
        "conv_b": nrm(ks[9], (DEPTH, D_CONV), 0.02),
        "ln_g": 1.0 + nrm(ks[10], (DEPTH, D_CONV), 0.02),
        "ln_b": nrm(ks[11], (DEPTH, D_CONV), 0.02),
        "w_conv_out": nrm(ks[12], (DEPTH, D_CONV, D_MODEL), D_CONV ** -0.5),
        "w_pool": nrm(ks[13], (DEPTH, N_POOL_GROUPS, POOL_GC, POOL_GC), POOL_GC ** -0.5),
        "pool_scale": 1.0 + nrm(ks[14], (DEPTH, D_POOL), 0.02),
        "w_out": nrm(ks[15], (DEPTH, D_MODEL, D_MODEL), D_MODEL ** -0.5),
        "norm2_g": 1.0 + nrm(ks[16], (DEPTH, D_MODEL), 0.02),
        "w_up": nrm(ks[17], (DEPTH, D_MODEL, 2 * D_FF), D_MODEL ** -0.5),
        "ffn_dw": nrm(ks[18], (DEPTH, FFN_CONV_W, 2 * D_FF), FFN_CONV_W ** -0.5),
        "w_down": nrm(ks[19], (DEPTH, D_FF, D_MODEL), D_FF ** -0.5),
        "final_g": 1.0 + nrm(ks[20], (D_MODEL,), 0.02),
    }


def reference(x_prompt, x_sample, state_conv, state_pool, state_ffn, meta_tokens,
              norm1_g, w_in, conv_dw, conv_b, ln_g, ln_b, w_conv_out, w_pool, pool_scale,
              w_out, norm2_g, w_up, ffn_dw, w_down, final_g):
    weights = (norm1_g, w_in, conv_dw, conv_b, ln_g, ln_b, w_conv_out, w_pool, pool_scale,
               w_out, norm2_g, w_up, ffn_dw, w_down, final_g)
    dt = x_prompt.dtype
    b = x_prompt.shape[0]
    meta = jnp.broadcast_to(meta_tokens.astype(dt)[None], (b, N_META, D_MODEL))
    xp = jnp.concatenate([meta, x_prompt], axis=1)
    z_conv = jnp.zeros((DEPTH, b, CONV_W - 1, D_CONV), dt)
    z_pool = jnp.zeros((DEPTH, b, POOL_MAX - 1, D_POOL), dt)
    z_ffn = jnp.zeros((DEPTH, b, FFN_CONV_W - 1, 2 * D_FF), dt)
    yp, conv_p, pool_p, ffn_p = run_trunk(xp, z_conv, z_pool, z_ffn, 0, *weights)
    y_prompt = yp[:, N_META:]
    y_sample, conv_s, pool_s, ffn_s = run_trunk(x_sample, state_conv, state_pool, state_ffn,
                                                PAST_LEN, *weights)
    return (y_prompt, y_sample, conv_p, pool_p, ffn_p, conv_s, pool_s, ffn_s)
```

```python
import functools

import jax
import jax.numpy as jnp
from jax import lax
from jax.experimental import pallas as pl
from jax.experimental.pallas import tpu as pltpu

D_MODEL = 1024
CONV_W = 31
POOL_WINDOWS = (2, 4, 8, 16)
POOL_MAX = max(POOL_WINDOWS)
N_POOL_GROUPS = len(POOL_WINDOWS)
POOL_GC = D_MODEL // N_POOL_GROUPS
D_FF = 2816
FFN_CONV_W = 3
N_META = 16
EPS = 1e-6

SUBLANES = 8
LANES = 128
CONV_HALO = CONV_W - 1
POOL_HALO = POOL_MAX - 1
FFN_HALO = FFN_CONV_W - 1
COL_CHUNK = 256
CONV_ROWS_PER_ITER = 32
VMEM_LIMIT_BYTES = 56 * 1024 * 1024

_F32 = jnp.float32
_BF16 = jnp.bfloat16


def _dot(a, b):
    return jnp.dot(a, b, preferred_element_type=_F32)


def _rms(x, g):
    ms = jnp.mean(x * x, axis=-1, keepdims=True)
    return x * lax.rsqrt(ms + EPS) * g


def _mixer_kernel(x_ref, cst_ref, pst_ref, g1_ref, win_ref, wdw_ref, cb_ref, lng_ref, lnb_ref,
                  wco_ref, wpool_ref, psc_ref, wout_ref,
                  xo_ref, nconv_ref, npool_ref,
                  h_ref, extc_ref, extp_ref, conv_ref, *, nb, tt, nt, pos0):
    d = D_MODEL
    tm = tt * nb
    hc = CONV_HALO * nb
    hp = POOL_HALO * nb
    t_idx = pl.program_id(1)

    @pl.when(t_idx == 0)
    def _():
        extc_ref[0:hc, :] = cst_ref[...]
        extp_ref[0:hp, :] = pst_ref[...]

    x = x_ref[...]
    h_ref[...] = _rms(x, g1_ref[...]).astype(_BF16)

    for c in range(d // COL_CHUNK):
        lo, hi = c * COL_CHUNK, (c + 1) * COL_CHUNK
        za = _dot(h_ref[...], win_ref[:, lo:hi])
        zg = _dot(h_ref[...], win_ref[:, d + lo:d + hi])
        extc_ref[hc:hc + tm, lo:hi] = za * jax.nn.sigmoid(zg)

    n_iter = tm // CONV_ROWS_PER_ITER
    strips = CONV_ROWS_PER_ITER // SUBLANES
    for lc in range(d // LANES):
        l0, l1 = lc * LANES, (lc + 1) * LANES
        w = [wdw_ref[k, :, l0:l1] for k in range(CONV_W)]
        bias = cb_ref[:, l0:l1]

        def body(i, carry, w=w, bias=bias, l0=l0, l1=l1):
            r0 = pl.multiple_of(i * CONV_ROWS_PER_ITER, CONV_ROWS_PER_ITER)
            for s in range(strips):
                rs = r0 + s * SUBLANES
                acc = w[0] * extc_ref[pl.ds(rs, SUBLANES), l0:l1]
                for k in range(1, CONV_W):
                    acc = acc + w[k] * extc_ref[pl.ds(rs + k * nb, SUBLANES), l0:l1]
                conv_ref[pl.ds(rs, SUBLANES), l0:l1] = acc + bias
            return carry

        lax.fori_loop(0, n_iter, body, 0)

    a = conv_ref[...]
    mu = jnp.mean(a, axis=-1, keepdims=True)
    ac = a - mu
    var = jnp.mean(ac * ac, axis=-1, keepdims=True)
    a = ac * lax.rsqrt(var + EPS) * lng_ref[...] + lnb_ref[...]
    a = a * jax.nn.sigmoid(a)
    a_out = _dot(a.astype(_BF16), wco_ref[...])

    extp_ref[hp:hp + tm, :] = _dot(h_ref[...], win_ref[:, 2 * d:3 * d])
    if pos0 + 1 < POOL_MAX:
        row = lax.broadcasted_iota(jnp.int32, (tm, 1), 0)
        pos = pos0 + t_idx * tt + row // nb
    p_parts = []
    for gi, win in enumerate(POOL_WINDOWS):
        l0, l1 = gi * POOL_GC, (gi + 1) * POOL_GC
        cur = extp_ref[hp:hp + tm, l0:l1]
        s = cur
        for j in range(1, win):
            s = s + extp_ref[hp - j * nb:hp - j * nb + tm, l0:l1]
        if pos0 + 1 < POOL_MAX:
            mean = s / jnp.minimum(pos + 1, win).astype(_F32)
        else:
            mean = s * (1.0 / win)
        p = (mean - cur).astype(_BF16)
        p_parts.append(_dot(p, wpool_ref[gi]) * psc_ref[:, l0:l1])
    p_out = jnp.concatenate(p_parts, axis=-1)

    ga = _dot(h_ref[...], win_ref[:, 3 * d:4 * d])
    gb = _dot(h_ref[...], win_ref[:, 4 * d:5 * d])
    m = jax.nn.sigmoid(ga) * a_out + jax.nn.sigmoid(gb) * p_out
    xo_ref[...] = x + _dot(m.astype(_BF16), wout_ref[...])

    if nt > 1:
        @pl.when(t_idx < nt - 1)
        def _():
            extc_ref[0:hc, :] = extc_ref[tm:tm + hc, :]
            extp_ref[0:hp, :] = extp_ref[tm:tm + hp, :]

    @pl.when(t_idx == nt - 1)
    def _():
        nconv_ref[...] = extc_ref[tm:tm + hc, :]
        npool_ref[...] = extp_ref[tm:tm + hp, :]


def _ffn_kernel(x_ref, fst_ref, g2_ref, wup_ref, fdw_ref, wdown_ref, fg_ref,
                xo_ref, nffn_ref,
                h_ref, tail_ref, *, nb, tt, nt, final_norm):
    f = D_FF
    tm = tt * nb
    hf = FFN_HALO * nb
    t_idx = pl.program_id(1)

    @pl.when(t_idx == 0)
    def _():
        tail_ref[...] = fst_ref[...]

    x = x_ref[...]
    h_ref[...] = _rms(x, g2_ref[...]).astype(_BF16)

    def conv3(up, lo, hi):
        ext = jnp.concatenate([tail_ref[:, lo:hi], up], axis=0)
        y = fdw_ref[0:1, lo:hi] * ext[0:tm]
        for k in range(1, FFN_CONV_W):
            y = y + fdw_ref[k:k + 1, lo:hi] * ext[k * nb:k * nb + tm]
        tail_ref[:, lo:hi] = ext[tm:tm + hf]
        return y

    acc = jnp.zeros((tm, D_MODEL), _F32)
    for c in range(f // COL_CHUNK):
        lo, hi = c * COL_CHUNK, (c + 1) * COL_CHUNK
        yv = conv3(_dot(h_ref[...], wup_ref[:, lo:hi]), lo, hi)
        yg = conv3(_dot(h_ref[...], wup_ref[:, f + lo:f + hi]), f + lo, f + hi)
        act = (jax.nn.gelu(yg) * yv).astype(_BF16)
        acc = acc + _dot(act, wdown_ref[lo:hi, :])
    y = x + acc
    if final_norm:
        y = _rms(y, fg_ref[...])
    xo_ref[...] = y

    @pl.when(t_idx == nt - 1)
    def _():
        nffn_ref[...] = tail_ref[...]


def _const_spec(shape):
    zeros = (0,) * len(shape)
    return pl.BlockSpec(shape, lambda g, t: zeros, pipeline_mode=pl.Buffered(1))


def _group_spec(rows, cols):
    return pl.BlockSpec((None, rows, cols), lambda g, t: (g, 0, 0))


def _tile_spec(rows, cols):
    return pl.BlockSpec((None, rows, cols), lambda g, t: (g, t, 0))


def _compiler_params():
    return pltpu.CompilerParams(dimension_semantics=("arbitrary", "arbitrary"),
                                vmem_limit_bytes=VMEM_LIMIT_BYTES)


def _mixer_call(x, cst, pst, wl, *, nb, tt, pos0, name):
    groups, rows, d = x.shape
    tm = tt * nb
    nt = rows // tm
    assert rows == nt * tm and tm % CONV_ROWS_PER_ITER == 0
    hc, hp = CONV_HALO * nb, POOL_HALO * nb
    assert nt == 1 or tm >= hc
    kern = functools.partial(_mixer_kernel, nb=nb, tt=tt, nt=nt, pos0=pos0)
    weights = (wl["norm1_g"], wl["w_in"], wl["conv_dw"], wl["conv_b"], wl["ln_g"], wl["ln_b"],
               wl["w_conv_out"], wl["w_pool"], wl["pool_scale"], wl["w_out"])
    return pl.pallas_call(
        kern,
        grid=(groups, nt),
        in_specs=[_tile_spec(tm, d), _group_spec(hc, d), _group_spec(hp, d)]
        + [_const_spec(w.shape) for w in weights],
        out_specs=[_tile_spec(tm, d), _group_spec(hc, d), _group_spec(hp, d)],
        out_shape=[jax.ShapeDtypeStruct((groups, rows, d), _F32),
                   jax.ShapeDtypeStruct((groups, hc, d), _F32),
                   jax.ShapeDtypeStruct((groups, hp, d), _F32)],
        scratch_shapes=[pltpu.VMEM((tm, d), _BF16),
                        pltpu.VMEM((hc + tm, d), _F32),
                        pltpu.VMEM((hp + tm, d), _F32),
                        pltpu.VMEM((tm, d), _F32)],
        compiler_params=_compiler_params(),
        name=name,
    )(x, cst, pst, *weights)


def _ffn_call(x, fst, wl, final_g, *, nb, tt, final_norm, name):
    groups, rows, d = x.shape
    tm = tt * nb
    nt = rows // tm
    assert rows == nt * tm
    hf = FFN_HALO * nb
    assert nt == 1 or tm >= hf
    kern = functools.partial(_ffn_kernel, nb=nb, tt=tt, nt=nt, final_norm=final_norm)
    weights = (wl["norm2_g"], wl["w_up"], wl["ffn_dw"], wl["w_down"], final_g)
    return pl.pallas_call(
        kern,
        grid=(groups, nt),
        in_specs=[_tile_spec(tm, d), _group_spec(hf, 2 * D_FF)]
        + [_const_spec(w.shape) for w in weights],
        out_specs=[_tile_spec(tm, d), _group_spec(hf, 2 * D_FF)],
        out_shape=[jax.ShapeDtypeStruct((groups, rows, d), _F32),
                   jax.ShapeDtypeStruct((groups, hf, 2 * D_FF), _F32)],
        scratch_shapes=[pltpu.VMEM((tm, d), _BF16),
                        pltpu.VMEM((hf, 2 * D_FF), _F32)],
        compiler_params=_compiler_params(),
        name=name,
    )(x, fst, *weights)


def _trunk(x, st_conv, st_pool, st_ffn, layers, final_g, *, nb, tt, pos0, tag):
    depth = len(layers)
    ncs, nps, nfs = [], [], []
    for l, wl in enumerate(layers):
        x, nc, npool = _mixer_call(x, st_conv[l], st_pool[l], wl, nb=nb, tt=tt, pos0=pos0,
                                   name=f"mixer_{tag}_{l}")
        x, nf = _ffn_call(x, st_ffn[l], wl, final_g, nb=nb, tt=tt,
                          final_norm=(l == depth - 1), name=f"ffn_{tag}_{l}")
        ncs.append(nc)
        nps.append(npool)
        nfs.append(nf)
    return x, ncs, nps, nfs


def _to_time_major(a, nb):
    s, r, c = a.shape
    return a.reshape(s // nb, nb, r, c).transpose(0, 2, 1, 3).reshape(s // nb, r * nb, c)


def _from_time_major(a, nb):
    g, rn, c = a.shape
    r = rn // nb
    return a.reshape(g, r, nb, c).transpose(0, 2, 1, 3).reshape(g * nb, r, c)


PROMPT_TT = 64
SAMPLE_NB = 32


def kernel(x_prompt, x_sample, state_conv, state_pool, state_ffn, meta_tokens, norm1_g, w_in, conv_dw, conv_b, ln_g, ln_b, w_conv_out, w_pool, pool_scale, w_out, norm2_g, w_up, ffn_dw, w_down, final_g):
    depth = w_in.shape[0]
    batch, seq, d = x_prompt.shape
    dec_batch, dec_seq, _ = x_sample.shape
    past_len = 16384

    layers = []
    for l in range(depth):
        layers.append(dict(
            norm1_g=norm1_g[l][None], w_in=w_in[l].astype(_BF16),
            conv_dw=jnp.broadcast_to(conv_dw[l][:, None, :], (CONV_W, SUBLANES, d)),
            conv_b=conv_b[l][None], ln_g=ln_g[l][None], ln_b=ln_b[l][None],
            w_conv_out=w_conv_out[l].astype(_BF16), w_pool=w_pool[l].astype(_BF16),
            pool_scale=pool_scale[l][None], w_out=w_out[l].astype(_BF16),
            norm2_g=norm2_g[l][None], w_up=w_up[l].astype(_BF16), ffn_dw=ffn_dw[l],
            w_down=w_down[l].astype(_BF16)))
    fg = final_g[None]

    nbp = batch
    xm = jnp.broadcast_to(meta_tokens[:, None, :], (N_META, nbp, d)).reshape(1, N_META * nbp, d)
    zc = [jnp.zeros((1, CONV_HALO * nbp, d), _F32)] * depth
    zp = [jnp.zeros((1, POOL_HALO * nbp, d), _F32)] * depth
    zf = [jnp.zeros((1, FFN_HALO * nbp, 2 * D_FF), _F32)] * depth
    _, mc, mp, mf = _trunk(xm, zc, zp, zf, layers, fg, nb=nbp, tt=N_META, pos0=0, tag="meta")

    xp = _to_time_major(x_prompt, nbp)
    yp, pc, pp, pf = _trunk(xp, mc, mp, mf, layers, fg, nb=nbp, tt=PROMPT_TT, pos0=N_META,
                            tag="prompt")
    y_prompt = _from_time_major(yp, nbp)
    conv_p = jnp.stack([_from_time_major(a, nbp) for a in pc])
    pool_p = jnp.stack([_from_time_major(a, nbp) for a in pp])
    ffn_p = jnp.stack([_from_time_major(a, nbp) for a in pf])

    nbs = SAMPLE_NB
    xs = _to_time_major(x_sample, nbs)
    sc = [_to_time_major(state_conv[l], nbs) for l in range(depth)]
    sp = [_to_time_major(state_pool[l], nbs) for l in range(depth)]
    sf = [_to_time_major(state_ffn[l], nbs) for l in range(depth)]
    ys, c_s, p_s, f_s = _trunk(xs, sc, sp, sf, layers, fg, nb=nbs, tt=dec_seq, pos0=past_len,
                               tag="sample")
    y_sample = _from_time_major(ys, nbs)
    conv_s = jnp.stack([_from_time_major(a, nbs) for a in c_s])
    pool_s = jnp.stack([_from_time_major(a, nbs) for a in p_s])
    ffn_s = jnp.stack([_from_time_major(a, nbs) for a in f_s])

    return (y_prompt, y_sample, conv_p, pool_p, ffn_p, conv_s, pool_s, ffn_s)
```

```python
import functools

import jax
import jax.numpy as jnp
from jax import lax
from jax.experimental import pallas as pl
from jax.experimental.pallas import tpu as pltpu

D_MODEL = 1024
CONV_W = 31
POOL_WINDOWS = (2, 4, 8, 16)
POOL_MAX = max(POOL_WINDOWS)
N_POOL_GROUPS = len(POOL_WINDOWS)
POOL_GC = D_MODEL // N_POOL_GROUPS
D_FF = 2816
FFN_CONV_W = 3
N_META = 16
EPS = 1e-6

SUBLANES = 8
LANES = 128
CONV_HALO = CONV_W - 1
POOL_HALO = POOL_MAX - 1
FFN_HALO = FFN_CONV_W - 1
COL_CHUNK = 256
CONV_TIME_BLOCK = 16
VMEM_LIMIT_BYTES = 56 * 1024 * 1024

_F32 = jnp.float32
_BF16 = jnp.bfloat16


def _dot(a, b):
    return jnp.dot(a, b, preferred_element_type=_F32)


def _rms(x, g):
    ms = jnp.mean(x * x, axis=-1, keepdims=True)
    return x * lax.rsqrt(ms + EPS) * g


def _mixer_kernel(x_ref, cst_ref, pst_ref, g1_ref, win_ref, wdw_ref, cb_ref, lng_ref, lnb_ref,
                  wco_ref, wpool_ref, psc_ref, wout_ref,
                  xo_ref, nconv_ref, npool_ref,
                  h_ref, extc_ref, extp_ref, conv_ref, ga_ref, gb_ref, *, nb, tt, nt, pos0):
    d = D_MODEL
    tm = tt * nb
    hc = CONV_HALO * nb
    hp = POOL_HALO * nb
    t_idx = pl.program_id(1)

    @pl.when(t_idx == 0)
    def _():
        extc_ref[0:hc, :] = cst_ref[...]
        extp_ref[0:hp, :] = pst_ref[...]

    x = x_ref[...]
    h_ref[...] = _rms(x, g1_ref[...]).astype(_BF16)

    def glu(c):
        lo, hi = c * COL_CHUNK, (c + 1) * COL_CHUNK
        za = _dot(h_ref[...], win_ref[:, lo:hi])
        zg = _dot(h_ref[...], win_ref[:, d + lo:d + hi])
        extc_ref[hc:hc + tm, lo:hi] = za * jax.nn.sigmoid(zg)

    def conv(c):
        for l0 in range(c * COL_CHUNK, (c + 1) * COL_CHUNK, LANES):
            l1 = l0 + LANES
            w = [wdw_ref[k, :, l0:l1] for k in range(CONV_W)]
            bias = cb_ref[:, l0:l1]
            for r_off in range(0, nb, SUBLANES):
                for t0 in range(0, tt, CONV_TIME_BLOCK):
                    n_out = min(CONV_TIME_BLOCK, tt - t0)
                    acc = [None] * n_out
                    for j in range(n_out + CONV_W - 1):
                        r = (t0 + j) * nb + r_off
                        e = extc_ref[r:r + SUBLANES, l0:l1]
                        for s in range(n_out):
                            k = j - s
                            if 0 <= k < CONV_W:
                                term = w[k] * e
                                acc[s] = term if acc[s] is None else acc[s] + term
                    for s in range(n_out):
                        r = (t0 + s) * nb + r_off
                        conv_ref[r:r + SUBLANES, l0:l1] = acc[s] + bias

    n_chunks = d // COL_CHUNK
    glu(0)
    for c in range(n_chunks):
        if c + 1 < n_chunks:
            glu(c + 1)
        else:
            extp_ref[hp:hp + tm, :] = _dot(h_ref[...], win_ref[:, 2 * d:3 * d])
            ga_ref[...] = _dot(h_ref[...], win_ref[:, 3 * d:4 * d])
            gb_ref[...] = _dot(h_ref[...], win_ref[:, 4 * d:5 * d])
        conv(c)

    a = conv_ref[...]
    mu = jnp.mean(a, axis=-1, keepdims=True)
    ac = a - mu
    var = jnp.mean(ac * ac, axis=-1, keepdims=True)
    a = ac * lax.rsqrt(var + EPS) * lng_ref[...] + lnb_ref[...]
    a = a * jax.nn.sigmoid(a)
    a_out = _dot(a.astype(_BF16), wco_ref[...])

    if pos0 + 1 < POOL_MAX:
        row = lax.broadcasted_iota(jnp.int32, (tm, 1), 0)
        pos = pos0 + t_idx * tt + row // nb
    p_parts = []
    for gi, win in enumerate(POOL_WINDOWS):
        l0, l1 = gi * POOL_GC, (gi + 1) * POOL_GC
        cur = extp_ref[hp:hp + tm, l0:l1]
        s = cur
        for j in range(1, win):
            s = s + extp_ref[hp - j * nb:hp - j * nb + tm, l0:l1]
        if pos0 + 1 < POOL_MAX:
            mean = s / jnp.minimum(pos + 1, win).astype(_F32)
        else:
            mean = s * (1.0 / win)
        p = (mean - cur).astype(_BF16)
        p_parts.append(_dot(p, wpool_ref[gi]) * psc_ref[:, l0:l1])
    p_out = jnp.concatenate(p_parts, axis=-1)

    m = jax.nn.sigmoid(ga_ref[...]) * a_out + jax.nn.sigmoid(gb_ref[...]) * p_out
    xo_ref[...] = x + _dot(m.astype(_BF16), wout_ref[...])

    if nt > 1:
        @pl.when(t_idx < nt - 1)
        def _():
            extc_ref[0:hc, :] = extc_ref[tm:tm + hc, :]
            extp_ref[0:hp, :] = extp_ref[tm:tm + hp, :]

    @pl.when(t_idx == nt - 1)
    def _():
        nconv_ref[...] = extc_ref[tm:tm + hc, :]
        npool_ref[...] = extp_ref[tm:tm + hp, :]


def _ffn_kernel(x_ref, fst_ref, g2_ref, wup_ref, fdw_ref, wdown_ref, fg_ref,
                xo_ref, nffn_ref,
                h_ref, tail_ref, *, nb, tt, nt, final_norm):
    f = D_FF
    tm = tt * nb
    hf = FFN_HALO * nb
    t_idx = pl.program_id(1)

    @pl.when(t_idx == 0)
    def _():
        tail_ref[...] = fst_ref[...]

    x = x_ref[...]
    h_ref[...] = _rms(x, g2_ref[...]).astype(_BF16)

    def conv3(up, lo, hi):
        ext = jnp.concatenate([tail_ref[:, lo:hi], up], axis=0)
        y = fdw_ref[0:1, lo:hi] * ext[0:tm]
        for k in range(1, FFN_CONV_W):
            y = y + fdw_ref[k:k + 1, lo:hi] * ext[k * nb:k * nb + tm]
        tail_ref[:, lo:hi] = ext[tm:tm + hf]
        return y

    acc = jnp.zeros((tm, D_MODEL), _F32)
    for c in range(f // COL_CHUNK):
        lo, hi = c * COL_CHUNK, (c + 1) * COL_CHUNK
        yv = conv3(_dot(h_ref[...], wup_ref[:, lo:hi]), lo, hi)
        yg = conv3(_dot(h_ref[...], wup_ref[:, f + lo:f + hi]), f + lo, f + hi)
        act = (jax.nn.gelu(yg) * yv).astype(_BF16)
        acc = acc + _dot(act, wdown_ref[lo:hi, :])
    y = x + acc
    if final_norm:
        y = _rms(y, fg_ref[...])
    xo_ref[...] = y

    @pl.when(t_idx == nt - 1)
    def _():
        nffn_ref[...] = tail_ref[...]


def _const_spec(shape):
    zeros = (0,) * len(shape)
    return pl.BlockSpec(shape, lambda g, t: zeros, pipeline_mode=pl.Buffered(1))


def _group_spec(rows, cols):
    return pl.BlockSpec((None, rows, cols), lambda g, t: (g, 0, 0))


def _tile_spec(rows, cols):
    return pl.BlockSpec((None, rows, cols), lambda g, t: (g, t, 0))


def _compiler_params():
    return pltpu.CompilerParams(dimension_semantics=("arbitrary", "arbitrary"),
                                vmem_limit_bytes=VMEM_LIMIT_BYTES)


def _mixer_call(x, cst, pst, wl, *, nb, tt, pos0, name):
    groups, rows, d = x.shape
    tm = tt * nb
    nt = rows // tm
    assert rows == nt * tm
    hc, hp = CONV_HALO * nb, POOL_HALO * nb
    assert nt == 1 or tm >= hc
    kern = functools.partial(_mixer_kernel, nb=nb, tt=tt, nt=nt, pos0=pos0)
    weights = (wl["norm1_g"], wl["w_in"], wl["conv_dw"], wl["conv_b"], wl["ln_g"], wl["ln_b"],
               wl["w_conv_out"], wl["w_pool"], wl["pool_scale"], wl["w_out"])
    return pl.pallas_call(
        kern,
        grid=(groups, nt),
        in_specs=[_tile_spec(tm, d), _group_spec(hc, d), _group_spec(hp, d)]
        + [_const_spec(w.shape) for w in weights],
        out_specs=[_tile_spec(tm, d), _group_spec(hc, d), _group_spec(hp, d)],
        out_shape=[jax.ShapeDtypeStruct((groups, rows, d), _F32),
                   jax.ShapeDtypeStruct((groups, hc, d), _F32),
                   jax.ShapeDtypeStruct((groups, hp, d), _F32)],
        scratch_shapes=[pltpu.VMEM((tm, d), _BF16),
                        pltpu.VMEM((hc + tm, d), _F32),
                        pltpu.VMEM((hp + tm, d), _F32),
                        pltpu.VMEM((tm, d), _F32),
                        pltpu.VMEM((tm, d), _F32),
                        pltpu.VMEM((tm, d), _F32)],
        compiler_params=_compiler_params(),
        name=name,
    )(x, cst, pst, *weights)


def _ffn_call(x, fst, wl, final_g, *, nb, tt, final_norm, name):
    groups, rows, d = x.shape
    tm = tt * nb
    nt = rows // tm
    assert rows == nt * tm
    hf = FFN_HALO * nb
    assert nt == 1 or tm >= hf
    kern = functools.partial(_ffn_kernel, nb=nb, tt=tt, nt=nt, final_norm=final_norm)
    weights = (wl["norm2_g"], wl["w_up"], wl["ffn_dw"], wl["w_down"], final_g)
    return pl.pallas_call(
        kern,
        grid=(groups, nt),
        in_specs=[_tile_spec(tm, d), _group_spec(hf, 2 * D_FF)]
        + [_const_spec(w.shape) for w in weights],
        out_specs=[_tile_spec(tm, d), _group_spec(hf, 2 * D_FF)],
        out_shape=[jax.ShapeDtypeStruct((groups, rows, d), _F32),
                   jax.ShapeDtypeStruct((groups, hf, 2 * D_FF), _F32)],
        scratch_shapes=[pltpu.VMEM((tm, d), _BF16),
                        pltpu.VMEM((hf, 2 * D_FF), _F32)],
        compiler_params=_compiler_params(),
        name=name,
    )(x, fst, *weights)


def _trunk(x, st_conv, st_pool, st_ffn, layers, final_g, *, nb, tt, pos0, tag):
    depth = len(layers)
    ncs, nps, nfs = [], [], []
    for l, wl in enumerate(layers):
        x, nc, npool = _mixer_call(x, st_conv[l], st_pool[l], wl, nb=nb, tt=tt, pos0=pos0,
                                   name=f"mixer_{tag}_{l}")
        x, nf = _ffn_call(x, st_ffn[l], wl, final_g, nb=nb, tt=tt,
                          final_norm=(l == depth - 1), name=f"ffn_{tag}_{l}")
        ncs.append(nc)
        nps.append(npool)
        nfs.append(nf)
    return x, ncs, nps, nfs


def _to_time_major(a, nb):
    s, r, c = a.shape
    return a.reshape(s // nb, nb, r, c).transpose(0, 2, 1, 3).reshape(s // nb, r * nb, c)


def _from_time_major(a, nb):
    g, rn, c = a.shape
    r = rn // nb
    return a.reshape(g, r, nb, c).transpose(0, 2, 1, 3).reshape(g * nb, r, c)


PROMPT_TT = 64
SAMPLE_NB = 32


def kernel(x_prompt, x_sample, state_conv, state_pool, state_ffn, meta_tokens, norm1_g, w_in, conv_dw, conv_b, ln_g, ln_b, w_conv_out, w_pool, pool_scale, w_out, norm2_g, w_up, ffn_dw, w_down, final_g):
    depth = w_in.shape[0]
    batch, seq, d = x_prompt.shape
    dec_batch, dec_seq, _ = x_sample.shape
    past_len = 16384

    layers = []
    for l in range(depth):
        layers.append(dict(
            norm1_g=norm1_g[l][None], w_in=w_in[l].astype(_BF16),
            conv_dw=jnp.broadcast_to(conv_dw[l][:, None, :], (CONV_W, SUBLANES, d)),
            conv_b=conv_b[l][None], ln_g=ln_g[l][None], ln_b=ln_b[l][None],
            w_conv_out=w_conv_out[l].astype(_BF16), w_pool=w_pool[l].astype(_BF16),
            pool_scale=pool_scale[l][None], w_out=w_out[l].astype(_BF16),
            norm2_g=norm2_g[l][None], w_up=w_up[l].astype(_BF16), ffn_dw=ffn_dw[l],
            w_down=w_down[l].astype(_BF16)))
    fg = final_g[None]

    nbp = batch
    xm = jnp.broadcast_to(meta_tokens[:, None, :], (N_META, nbp, d)).reshape(1, N_META * nbp, d)
    zc = [jnp.zeros((1, CONV_HALO * nbp, d), _F32)] * depth
    zp = [jnp.zeros((1, POOL_HALO * nbp, d), _F32)] * depth
    zf = [jnp.zeros((1, FFN_HALO * nbp, 2 * D_FF), _F32)] * depth
    _, mc, mp, mf = _trunk(xm, zc, zp, zf, layers, fg, nb=nbp, tt=N_META, pos0=0, tag="meta")

    xp = _to_time_major(x_prompt, nbp)
    yp, pc, pp, pf = _trunk(xp, mc, mp, mf, layers, fg, nb=nbp, tt=PROMPT_TT, pos0=N_META,
                            tag="prompt")
    y_prompt = _from_time_major(yp, nbp)
    conv_p = jnp.stack([_from_time_major(a, nbp) for a in pc])
    pool_p = jnp.stack([_from_time_major(a, nbp) for a in pp])
    ffn_p = jnp.stack([_from_time_major(a, nbp) for a in pf])

    nbs = SAMPLE_NB
    xs = _to_time_major(x_sample, nbs)
    sc = [_to_time_major(state_conv[l], nbs) for l in range(depth)]
    sp = [_to_time_major(state_pool[l], nbs) for l in range(depth)]
    sf = [_to_time_major(state_ffn[l], nbs) for l in range(depth)]
    ys, c_s, p_s, f_s = _trunk(xs, sc, sp, sf, layers, fg, nb=nbs, tt=dec_seq, pos0=past_len,
                               tag="sample")
    y_sample = _from_time_major(ys, nbs)
    conv_s = jnp.stack([_from_time_major(a, nbs) for a in c_s])
    pool_s = jnp.stack([_from_time_major(a, nbs) for a in p_s])
    ffn_s = jnp.stack([_from_time_major(a, nbs) for a in f_s])

    return (y_prompt, y_sample, conv_p, pool_p, ffn_p, conv_s, pool_s, ffn_s)
```

```python
import functools

import jax
import jax.numpy as jnp
from jax import lax
from jax.experimental import pallas as pl
from jax.experimental.pallas import tpu as pltpu

D_MODEL = 1024
CONV_W = 31
POOL_WINDOWS = (2, 4, 8, 16)
POOL_MAX = max(POOL_WINDOWS)
N_POOL_GROUPS = len(POOL_WINDOWS)
POOL_GC = D_MODEL // N_POOL_GROUPS
D_FF = 2816
FFN_CONV_W = 3
N_META = 16
EPS = 1e-6

SUBLANES = 8
LANES = 128
CONV_HALO = CONV_W - 1
POOL_HALO = POOL_MAX - 1
FFN_HALO = FFN_CONV_W - 1
COL_CHUNK = 256
CONV_TIME_BLOCK = 16
VMEM_LIMIT_BYTES = 56 * 1024 * 1024

_F32 = jnp.float32
_BF16 = jnp.bfloat16


def _dot(a, b):
    return jnp.dot(a, b, preferred_element_type=_F32)


def _rms(x, g):
    ms = jnp.mean(x * x, axis=-1, keepdims=True)
    return x * lax.rsqrt(ms + EPS) * g


def _mixer_kernel(x_ref, cst_ref, pst_ref, g1_ref, win_ref, wdw_ref, cb_ref, lng_ref, lnb_ref,
                  wco_ref, wpool_ref, psc_ref, wout_ref,
                  xo_ref, nconv_ref, npool_ref,
                  h_ref, extc_ref, extp_ref, conv_ref, ga_ref, gb_ref, *, nb, tt, nt, pos0):
    d = D_MODEL
    tm = tt * nb
    hc = CONV_HALO * nb
    hp = POOL_HALO * nb
    t_idx = pl.program_id(1)

    @pl.when(t_idx == 0)
    def _():
        extc_ref[0:hc, :] = cst_ref[...]
        extp_ref[0:hp, :] = pst_ref[...]

    x = x_ref[...]
    h_ref[...] = _rms(x, g1_ref[...]).astype(_BF16)

    def glu(c):
        lo, hi = c * COL_CHUNK, (c + 1) * COL_CHUNK
        za = _dot(h_ref[...], win_ref[:, lo:hi])
        zg = _dot(h_ref[...], win_ref[:, d + lo:d + hi])
        extc_ref[hc:hc + tm, lo:hi] = za * jax.nn.sigmoid(zg)

    def conv(c):
        for l0 in range(c * COL_CHUNK, (c + 1) * COL_CHUNK, LANES):
            l1 = l0 + LANES
            w = [wdw_ref[k, :, l0:l1] for k in range(CONV_W)]
            bias = cb_ref[:, l0:l1]
            for r_off in range(0, nb, SUBLANES):
                for t0 in range(0, tt, CONV_TIME_BLOCK):
                    n_out = min(CONV_TIME_BLOCK, tt - t0)
                    acc = [None] * n_out
                    for j in range(n_out + CONV_W - 1):
                        r = (t0 + j) * nb + r_off
                        e = extc_ref[r:r + SUBLANES, l0:l1]
                        for s in range(n_out):
                            k = j - s
                            if 0 <= k < CONV_W:
                                term = w[k] * e
                                acc[s] = term if acc[s] is None else acc[s] + term
                    for s in range(n_out):
                        r = (t0 + s) * nb + r_off
                        conv_ref[r:r + SUBLANES, l0:l1] = acc[s] + bias

    n_chunks = d // COL_CHUNK
    glu(0)
    for c in range(n_chunks):
        if c + 1 < n_chunks:
            glu(c + 1)
        else:
            extp_ref[hp:hp + tm, :] = _dot(h_ref[...], win_ref[:, 2 * d:3 * d])
            ga_ref[...] = _dot(h_ref[...], win_ref[:, 3 * d:4 * d])
            gb_ref[...] = _dot(h_ref[...], win_ref[:, 4 * d:5 * d])
        conv(c)

    a = conv_ref[...]
    mu = jnp.mean(a, axis=-1, keepdims=True)
    ac = a - mu
    var = jnp.mean(ac * ac, axis=-1, keepdims=True)
    a = ac * lax.rsqrt(var + EPS) * lng_ref[...] + lnb_ref[...]
    a = a * jax.nn.sigmoid(a)
    a_out = _dot(a.astype(_BF16), wco_ref[...])

    if pos0 + 1 < POOL_MAX:
        row = lax.broadcasted_iota(jnp.int32, (tm, 1), 0)
        pos = pos0 + t_idx * tt + row // nb
    p_parts = []
    for gi, win in enumerate(POOL_WINDOWS):
        l0, l1 = gi * POOL_GC, (gi + 1) * POOL_GC
        cur = extp_ref[hp:hp + tm, l0:l1]
        s = cur
        for j in range(1, win):
            s = s + extp_ref[hp - j * nb:hp - j * nb + tm, l0:l1]
        if pos0 + 1 < POOL_MAX:
            mean = s / jnp.minimum(pos + 1, win).astype(_F32)
        else:
            mean = s * (1.0 / win)
        p = (mean - cur).astype(_BF16)
        p_parts.append(_dot(p, wpool_ref[gi]) * psc_ref[:, l0:l1])
    p_out = jnp.concatenate(p_parts, axis=-1)

    m = jax.nn.sigmoid(ga_ref[...]) * a_out + jax.nn.sigmoid(gb_ref[...]) * p_out
    xo_ref[...] = x + _dot(m.astype(_BF16), wout_ref[...])

    if nt > 1:
        @pl.when(t_idx < nt - 1)
        def _():
            extc_ref[0:hc, :] = extc_ref[tm:tm + hc, :]
            extp_ref[0:hp, :] = extp_ref[tm:tm + hp, :]

    @pl.when(t_idx == nt - 1)
    def _():
        nconv_ref[...] = extc_ref[tm:tm + hc, :]
        npool_ref[...] = extp_ref[tm:tm + hp, :]


def _ffn_kernel(x_ref, fst_ref, g2_ref, wup_ref, fdw_ref, wdown_ref, fg_ref,
                xo_ref, nffn_ref,
                h_ref, tail_ref, act_ref, *, nb, tt, nt, final_norm):
    f = D_FF
    tm = tt * nb
    hf = FFN_HALO * nb
    t_idx = pl.program_id(1)

    @pl.when(t_idx == 0)
    def _():
        tail_ref[...] = fst_ref[...]

    x = x_ref[...]
    h_ref[...] = _rms(x, g2_ref[...]).astype(_BF16)

    def conv3(up, lo, hi):
        ext = jnp.concatenate([tail_ref[:, lo:hi], up], axis=0)
        y = fdw_ref[0:1, lo:hi] * ext[0:tm]
        for k in range(1, FFN_CONV_W):
            y = y + fdw_ref[k:k + 1, lo:hi] * ext[k * nb:k * nb + tm]
        tail_ref[:, lo:hi] = ext[tm:tm + hf]
        return y

    for c in range(f // COL_CHUNK):
        lo, hi = c * COL_CHUNK, (c + 1) * COL_CHUNK
        yv = conv3(_dot(h_ref[...], wup_ref[:, lo:hi]), lo, hi)
        yg = conv3(_dot(h_ref[...], wup_ref[:, f + lo:f + hi]), f + lo, f + hi)
        act_ref[:, lo:hi] = (jax.nn.gelu(yg) * yv).astype(_BF16)
    y = x + _dot(act_ref[...], wdown_ref[...])
    if final_norm:
        y = _rms(y, fg_ref[...])
    xo_ref[...] = y

    @pl.when(t_idx == nt - 1)
    def _():
        nffn_ref[...] = tail_ref[...]


def _const_spec(shape):
    zeros = (0,) * len(shape)
    return pl.BlockSpec(shape, lambda g, t: zeros, pipeline_mode=pl.Buffered(1))


def _group_spec(rows, cols):
    return pl.BlockSpec((None, rows, cols), lambda g, t: (g, 0, 0))


def _tile_spec(rows, cols):
    return pl.BlockSpec((None, rows, cols), lambda g, t: (g, t, 0))


def _compiler_params():
    return pltpu.CompilerParams(dimension_semantics=("arbitrary", "arbitrary"),
                                vmem_limit_bytes=VMEM_LIMIT_BYTES)


def _mixer_call(x, cst, pst, wl, *, nb, tt, pos0, name):
    groups, rows, d = x.shape
    tm = tt * nb
    nt = rows // tm
    assert rows == nt * tm
    hc, hp = CONV_HALO * nb, POOL_HALO * nb
    assert nt == 1 or tm >= hc
    kern = functools.partial(_mixer_kernel, nb=nb, tt=tt, nt=nt, pos0=pos0)
    weights = (wl["norm1_g"], wl["w_in"], wl["conv_dw"], wl["conv_b"], wl["ln_g"], wl["ln_b"],
               wl["w_conv_out"], wl["w_pool"], wl["pool_scale"], wl["w_out"])
    return pl.pallas_call(
        kern,
        grid=(groups, nt),
        in_specs=[_tile_spec(tm, d), _group_spec(hc, d), _group_spec(hp, d)]
        + [_const_spec(w.shape) for w in weights],
        out_specs=[_tile_spec(tm, d), _group_spec(hc, d), _group_spec(hp, d)],
        out_shape=[jax.ShapeDtypeStruct((groups, rows, d), _F32),
                   jax.ShapeDtypeStruct((groups, hc, d), _F32),
                   jax.ShapeDtypeStruct((groups, hp, d), _F32)],
        scratch_shapes=[pltpu.VMEM((tm, d), _BF16),
                        pltpu.VMEM((hc + tm, d), _F32),
                        pltpu.VMEM((hp + tm, d), _F32),
                        pltpu.VMEM((tm, d), _F32),
                        pltpu.VMEM((tm, d), _F32),
                        pltpu.VMEM((tm, d), _F32)],
        compiler_params=_compiler_params(),
        name=name,
    )(x, cst, pst, *weights)


def _ffn_call(x, fst, wl, final_g, *, nb, tt, final_norm, name):
    groups, rows, d = x.shape
    tm = tt * nb
    nt = rows // tm
    assert rows == nt * tm
    hf = FFN_HALO * nb
    assert nt == 1 or tm >= hf
    kern = functools.partial(_ffn_kernel, nb=nb, tt=tt, nt=nt, final_norm=final_norm)
    weights = (wl["norm2_g"], wl["w_up"], wl["ffn_dw"], wl["w_down"], final_g)
    return pl.pallas_call(
        kern,
        grid=(groups, nt),
        in_specs=[_tile_spec(tm, d), _group_spec(hf, 2 * D_FF)]
        + [_const_spec(w.shape) for w in weights],
        out_specs=[_tile_spec(tm, d), _group_spec(hf, 2 * D_FF)],
        out_shape=[jax.ShapeDtypeStruct((groups, rows, d), _F32),
                   jax.ShapeDtypeStruct((groups, hf, 2 * D_FF), _F32)],
        scratch_shapes=[pltpu.VMEM((tm, d), _BF16),
                        pltpu.VMEM((hf, 2 * D_FF), _F32),
                        pltpu.VMEM((tm, D_FF), _BF16)],
        compiler_params=_compiler_params(),
        name=name,
    )(x, fst, *weights)


def _trunk(x, st_conv, st_pool, st_ffn, layers, final_g, *, nb, tt, pos0, tag):
    depth = len(layers)
    ncs, nps, nfs = [], [], []
    for l, wl in enumerate(layers):
        x, nc, npool = _mixer_call(x, st_conv[l], st_pool[l], wl, nb=nb, tt=tt, pos0=pos0,
                                   name=f"mixer_{tag}_{l}")
        x, nf = _ffn_call(x, st_ffn[l], wl, final_g, nb=nb, tt=tt,
                          final_norm=(l == depth - 1), name=f"ffn_{tag}_{l}")
        ncs.append(nc)
        nps.append(npool)
        nfs.append(nf)
    return x, ncs, nps, nfs


def _to_time_major(a, nb):
    s, r, c = a.shape
    return a.reshape(s // nb, nb, r, c).transpose(0, 2, 1, 3).reshape(s // nb, r * nb, c)


def _from_time_major(a, nb):
    g, rn, c = a.shape
    r = rn // nb
    return a.reshape(g, r, nb, c).transpose(0, 2, 1, 3).reshape(g * nb, r, c)


PROMPT_TT = 64
SAMPLE_NB = 32


def kernel(x_prompt, x_sample, state_conv, state_pool, state_ffn, meta_tokens, norm1_g, w_in, conv_dw, conv_b, ln_g, ln_b, w_conv_out, w_pool, pool_scale, w_out, norm2_g, w_up, ffn_dw, w_down, final_g):
    depth = w_in.shape[0]
    batch, seq, d = x_prompt.shape
    dec_batch, dec_seq, _ = x_sample.shape
    past_len = 16384

    layers = []
    for l in range(depth):
        layers.append(dict(
            norm1_g=norm1_g[l][None], w_in=w_in[l].astype(_BF16),
            conv_dw=jnp.broadcast_to(conv_dw[l][:, None, :], (CONV_W, SUBLANES, d)),
            conv_b=conv_b[l][None], ln_g=ln_g[l][None], ln_b=ln_b[l][None],
            w_conv_out=w_conv_out[l].astype(_BF16), w_pool=w_pool[l].astype(_BF16),
            pool_scale=pool_scale[l][None], w_out=w_out[l].astype(_BF16),
            norm2_g=norm2_g[l][None], w_up=w_up[l].astype(_BF16), ffn_dw=ffn_dw[l],
            w_down=w_down[l].astype(_BF16)))
    fg = final_g[None]

    nbp = batch
    xm = jnp.broadcast_to(meta_tokens[:, None, :], (N_META, nbp, d)).reshape(1, N_META * nbp, d)
    zc = [jnp.zeros((1, CONV_HALO * nbp, d), _F32)] * depth
    zp = [jnp.zeros((1, POOL_HALO * nbp, d), _F32)] * depth
    zf = [jnp.zeros((1, FFN_HALO * nbp, 2 * D_FF), _F32)] * depth
    _, mc, mp, mf = _trunk(xm, zc, zp, zf, layers, fg, nb=nbp, tt=N_META, pos0=0, tag="meta")

    xp = _to_time_major(x_prompt, nbp)
    yp, pc, pp, pf = _trunk(xp, mc, mp, mf, layers, fg, nb=nbp, tt=PROMPT_TT, pos0=N_META,
                            tag="prompt")
    y_prompt = _from_time_major(yp, nbp)
    conv_p = jnp.stack([_from_time_major(a, nbp) for a in pc])
    pool_p = jnp.stack([_from_time_major(a, nbp) for a in pp])
    ffn_p = jnp.stack([_from_time_major(a, nbp) for a in pf])

    nbs = SAMPLE_NB
    xs = _to_time_major(x_sample, nbs)
    sc = [_to_time_major(state_conv[l], nbs) for l in range(depth)]
    sp = [_to_time_major(state_pool[l], nbs) for l in range(depth)]
    sf = [_to_time_major(state_ffn[l], nbs) for l in range(depth)]
    ys, c_s, p_s, f_s = _trunk(xs, sc, sp, sf, layers, fg, nb=nbs, tt=dec_seq, pos0=past_len,
                               tag="sample")
    y_sample = _from_time_major(ys, nbs)
    conv_s = jnp.stack([_from_time_major(a, nbs) for a in c_s])
    pool_s = jnp.stack([_from_time_major(a, nbs) for a in p_s])
    ffn_s = jnp.stack([_from_time_major(a, nbs) for a in f_s])

    return (y_prompt, y_sample, conv_p, pool_p, ffn_p, conv_s, pool_s, ffn_s)
```

```python
import functools

import jax
import jax.numpy as jnp
from jax import lax
from jax.experimental import pallas as pl
from jax.experimental.pallas import tpu as pltpu

D_MODEL = 1024
CONV_W = 31
POOL_WINDOWS = (2, 4, 8, 16)
POOL_MAX = max(POOL_WINDOWS)
N_POOL_GROUPS = len(POOL_WINDOWS)
POOL_GC = D_MODEL // N_POOL_GROUPS
D_FF = 2816
FFN_CONV_W = 3
N_META = 16
EPS = 1e-6

SUBLANES = 8
LANES = 128
CONV_HALO = CONV_W - 1
POOL_HALO = POOL_MAX - 1
FFN_HALO = FFN_CONV_W - 1
COL_CHUNK = 256
CONV_TIME_BLOCK = 16
VMEM_LIMIT_BYTES = 56 * 1024 * 1024

_F32 = jnp.float32
_BF16 = jnp.bfloat16


def _dot(a, b):
    return jnp.dot(a, b, preferred_element_type=_F32)


def _rms(x, g):
    ms = jnp.mean(x * x, axis=-1, keepdims=True)
    return x * lax.rsqrt(ms + EPS) * g


def _to_rows(a):
    nb, r, c = a.shape
    return jnp.swapaxes(a, 0, 1).reshape(r * nb, c)


def _from_rows(a, nb):
    rows, c = a.shape
    return jnp.swapaxes(a.reshape(rows // nb, nb, c), 0, 1)


def _mixer_kernel(x_ref, cst_ref, pst_ref, g1_ref, win_ref, wdw_ref, cb_ref, lng_ref, lnb_ref,
                  wco_ref, wpool_ref, psc_ref, wout_ref,
                  xo_ref, nconv_ref, npool_ref,
                  h_ref, extc_ref, extp_ref, conv_ref, ga_ref, gb_ref,
                  *, nb, tt, nt, pos0, x_natural):
    d = D_MODEL
    tm = tt * nb
    hc = CONV_HALO * nb
    hp = POOL_HALO * nb
    t_idx = pl.program_id(1)

    @pl.when(t_idx == 0)
    def _():
        extc_ref[0:hc, :] = _to_rows(cst_ref[...])
        extp_ref[0:hp, :] = _to_rows(pst_ref[...])

    x = _to_rows(x_ref[...]) if x_natural else x_ref[...]
    h_ref[...] = _rms(x, g1_ref[...]).astype(_BF16)

    def glu(c):
        lo, hi = c * COL_CHUNK, (c + 1) * COL_CHUNK
        za = _dot(h_ref[...], win_ref[:, lo:hi])
        zg = _dot(h_ref[...], win_ref[:, d + lo:d + hi])
        extc_ref[hc:hc + tm, lo:hi] = za * jax.nn.sigmoid(zg)

    def conv(c):
        for l0 in range(c * COL_CHUNK, (c + 1) * COL_CHUNK, LANES):
            l1 = l0 + LANES
            w = [wdw_ref[k, :, l0:l1] for k in range(CONV_W)]
            bias = cb_ref[:, l0:l1]
            for r_off in range(0, nb, SUBLANES):
                for t0 in range(0, tt, CONV_TIME_BLOCK):
                    n_out = min(CONV_TIME_BLOCK, tt - t0)
                    acc = [None] * n_out
                    for j in range(n_out + CONV_W - 1):
                        r = (t0 + j) * nb + r_off
                        e = extc_ref[r:r + SUBLANES, l0:l1]
                        for s in range(n_out):
                            k = j - s
                            if 0 <= k < CONV_W:
                                term = w[k] * e
                                acc[s] = term if acc[s] is None else acc[s] + term
                    for s in range(n_out):
                        r = (t0 + s) * nb + r_off
                        conv_ref[r:r + SUBLANES, l0:l1] = acc[s] + bias

    n_chunks = d // COL_CHUNK
    glu(0)
    for c in range(n_chunks):
        if c + 1 < n_chunks:
            glu(c + 1)
        else:
            extp_ref[hp:hp + tm, :] = _dot(h_ref[...], win_ref[:, 2 * d:3 * d])
            ga_ref[...] = _dot(h_ref[...], win_ref[:, 3 * d:4 * d])
            gb_ref[...] = _dot(h_ref[...], win_ref[:, 4 * d:5 * d])
        conv(c)

    a = conv_ref[...]
    mu = jnp.mean(a, axis=-1, keepdims=True)
    ac = a - mu
    var = jnp.mean(ac * ac, axis=-1, keepdims=True)
    a = ac * lax.rsqrt(var + EPS) * lng_ref[...] + lnb_ref[...]
    a = a * jax.nn.sigmoid(a)
    a_out = _dot(a.astype(_BF16), wco_ref[...])

    if pos0 + 1 < POOL_MAX:
        row = lax.broadcasted_iota(jnp.int32, (tm, 1), 0)
        pos = pos0 + t_idx * tt + row // nb
    p_parts = []
    for gi, win in enumerate(POOL_WINDOWS):
        l0, l1 = gi * POOL_GC, (gi + 1) * POOL_GC
        cur = extp_ref[hp:hp + tm, l0:l1]
        s = cur
        for j in range(1, win):
            s = s + extp_ref[hp - j * nb:hp - j * nb + tm, l0:l1]
        if pos0 + 1 < POOL_MAX:
            mean = s / jnp.minimum(pos + 1, win).astype(_F32)
        else:
            mean = s * (1.0 / win)
        p = (mean - cur).astype(_BF16)
        p_parts.append(_dot(p, wpool_ref[gi]) * psc_ref[:, l0:l1])
    p_out = jnp.concatenate(p_parts, axis=-1)

    m = jax.nn.sigmoid(ga_ref[...]) * a_out + jax.nn.sigmoid(gb_ref[...]) * p_out
    xo_ref[...] = x + _dot(m.astype(_BF16), wout_ref[...])

    if nt > 1:
        @pl.when(t_idx < nt - 1)
        def _():
            extc_ref[0:hc, :] = extc_ref[tm:tm + hc, :]
            extp_ref[0:hp, :] = extp_ref[tm:tm + hp, :]

    @pl.when(t_idx == nt - 1)
    def _():
        nconv_ref[...] = _from_rows(extc_ref[tm:tm + hc, :], nb)
        npool_ref[...] = _from_rows(extp_ref[tm:tm + hp, :], nb)


def _ffn_kernel(x_ref, fst_ref, g2_ref, wup_ref, fdw_ref, wdown_ref, fg_ref,
                xo_ref, nffn_ref,
                h_ref, tail_ref, act_ref, *, nb, tt, nt, final_norm, y_natural):
    f = D_FF
    tm = tt * nb
    hf = FFN_HALO * nb
    t_idx = pl.program_id(1)

    @pl.when(t_idx == 0)
    def _():
        tail_ref[...] = _to_rows(fst_ref[...])

    x = x_ref[...]
    h_ref[...] = _rms(x, g2_ref[...]).astype(_BF16)

    def conv3(up, lo, hi):
        ext = jnp.concatenate([tail_ref[:, lo:hi], up], axis=0)
        y = fdw_ref[0:1, lo:hi] * ext[0:tm]
        for k in range(1, FFN_CONV_W):
            y = y + fdw_ref[k:k + 1, lo:hi] * ext[k * nb:k * nb + tm]
        tail_ref[:, lo:hi] = ext[tm:tm + hf]
        return y

    for c in range(f // COL_CHUNK):
        lo, hi = c * COL_CHUNK, (c + 1) * COL_CHUNK
        yv = conv3(_dot(h_ref[...], wup_ref[:, lo:hi]), lo, hi)
        yg = conv3(_dot(h_ref[...], wup_ref[:, f + lo:f + hi]), f + lo, f + hi)
        act_ref[:, lo:hi] = (jax.nn.gelu(yg) * yv).astype(_BF16)
    y = x + _dot(act_ref[...], wdown_ref[...])
    if final_norm:
        y = _rms(y, fg_ref[...])
    xo_ref[...] = _from_rows(y, nb) if y_natural else y

    @pl.when(t_idx == nt - 1)
    def _():
        nffn_ref[...] = _from_rows(tail_ref[...], nb)


def _const_spec(shape):
    zeros = (0,) * len(shape)
    return pl.BlockSpec(shape, lambda g, t: zeros, pipeline_mode=pl.Buffered(1))


def _state_in_spec(layer, nb, steps, cols):
    return pl.BlockSpec((None, nb, steps, cols), lambda g, t: (layer, g, 0, 0))


def _state_out_spec(nb, steps, cols):
    return pl.BlockSpec((nb, steps, cols), lambda g, t: (g, 0, 0))


def _natural_spec(nb, tt, cols):
    return pl.BlockSpec((nb, tt, cols), lambda g, t: (g, t, 0))


def _rows_spec(rows, cols):
    return pl.BlockSpec((None, rows, cols), lambda g, t: (g, t, 0))


def _compiler_params():
    return pltpu.CompilerParams(dimension_semantics=("arbitrary", "arbitrary"),
                                vmem_limit_bytes=VMEM_LIMIT_BYTES)


def _mixer_call(x, cst, pst, layer, wl, *, nb, tt, pos0, x_natural, name):
    d = D_MODEL
    seqs = cst.shape[1]
    groups = seqs // nb
    steps = x.shape[1] if x_natural else x.shape[1] // nb
    tm = tt * nb
    nt = steps // tt
    assert seqs == groups * nb and steps == nt * tt
    hc, hp = CONV_HALO * nb, POOL_HALO * nb
    assert nt == 1 or tm >= hc
    kern = functools.partial(_mixer_kernel, nb=nb, tt=tt, nt=nt, pos0=pos0, x_natural=x_natural)
    weights = (wl["norm1_g"], wl["w_in"], wl["conv_dw"], wl["conv_b"], wl["ln_g"], wl["ln_b"],
               wl["w_conv_out"], wl["w_pool"], wl["pool_scale"], wl["w_out"])
    x_spec = _natural_spec(nb, tt, d) if x_natural else _rows_spec(tm, d)
    return pl.pallas_call(
        kern,
        grid=(groups, nt),
        in_specs=[x_spec, _state_in_spec(layer, nb, CONV_HALO, d),
                  _state_in_spec(layer, nb, POOL_HALO, d)]
        + [_const_spec(w.shape) for w in weights],
        out_specs=[_rows_spec(tm, d), _state_out_spec(nb, CONV_HALO, d),
                   _state_out_spec(nb, POOL_HALO, d)],
        out_shape=[jax.ShapeDtypeStruct((groups, steps * nb, d), _F32),
                   jax.ShapeDtypeStruct((seqs, CONV_HALO, d), _F32),
                   jax.ShapeDtypeStruct((seqs, POOL_HALO, d), _F32)],
        scratch_shapes=[pltpu.VMEM((tm, d), _BF16),
                        pltpu.VMEM((hc + tm, d), _F32),
                        pltpu.VMEM((hp + tm, d), _F32),
                        pltpu.VMEM((tm, d), _F32),
                        pltpu.VMEM((tm, d), _F32),
                        pltpu.VMEM((tm, d), _F32)],
        compiler_params=_compiler_params(),
        name=name,
    )(x, cst, pst, *weights)


def _ffn_call(x, fst, layer, wl, final_g, *, nb, tt, final_norm, y_natural, name):
    d = D_MODEL
    groups, rows, _ = x.shape
    seqs = groups * nb
    steps = rows // nb
    tm = tt * nb
    nt = steps // tt
    assert steps == nt * tt
    hf = FFN_HALO * nb
    assert nt == 1 or tm >= hf
    kern = functools.partial(_ffn_kernel, nb=nb, tt=tt, nt=nt, final_norm=final_norm,
                             y_natural=y_natural)
    weights = (wl["norm2_g"], wl["w_up"], wl["ffn_dw"], wl["w_down"], final_g)
    if y_natural:
        y_spec, y_shape = _natural_spec(nb, tt, d), (seqs, steps, d)
    else:
        y_spec, y_shape = _rows_spec(tm, d), (groups, rows, d)
    return pl.pallas_call(
        kern,
        grid=(groups, nt),
        in_specs=[_rows_spec(tm, d), _state_in_spec(layer, nb, FFN_HALO, 2 * D_FF)]
        + [_const_spec(w.shape) for w in weights],
        out_specs=[y_spec, _state_out_spec(nb, FFN_HALO, 2 * D_FF)],
        out_shape=[jax.ShapeDtypeStruct(y_shape, _F32),
                   jax.ShapeDtypeStruct((seqs, FFN_HALO, 2 * D_FF), _F32)],
        scratch_shapes=[pltpu.VMEM((tm, d), _BF16),
                        pltpu.VMEM((hf, 2 * D_FF), _F32),
                        pltpu.VMEM((tm, D_FF), _BF16)],
        compiler_params=_compiler_params(),
        name=name,
    )(x, fst, *weights)


def _trunk(x, states, layers, final_g, *, nb, tt, pos0, tag):
    depth = len(layers)
    new_states = []
    for l, wl in enumerate(layers):
        cst, pst, fst, idx = states[l]
        x, nc, npool = _mixer_call(x, cst, pst, idx, wl, nb=nb, tt=tt, pos0=pos0,
                                   x_natural=(l == 0), name=f"mixer_{tag}_{l}")
        x, nf = _ffn_call(x, fst, idx, wl, final_g, nb=nb, tt=tt, final_norm=(l == depth - 1),
                          y_natural=(l == depth - 1), name=f"ffn_{tag}_{l}")
        new_states.append((nc, npool, nf))
    return x, new_states


PROMPT_TT = 64
SAMPLE_NB = 32


def kernel(x_prompt, x_sample, state_conv, state_pool, state_ffn, meta_tokens, norm1_g, w_in, conv_dw, conv_b, ln_g, ln_b, w_conv_out, w_pool, pool_scale, w_out, norm2_g, w_up, ffn_dw, w_down, final_g):
    depth = w_in.shape[0]
    batch, seq, d = x_prompt.shape
    dec_batch, dec_seq, _ = x_sample.shape
    past_len = 16384

    layers = []
    for l in range(depth):
        layers.append(dict(
            norm1_g=norm1_g[l][None], w_in=w_in[l].astype(_BF16),
            conv_dw=jnp.broadcast_to(conv_dw[l][:, None, :], (CONV_W, SUBLANES, d)),
            conv_b=conv_b[l][None], ln_g=ln_g[l][None], ln_b=ln_b[l][None],
            w_conv_out=w_conv_out[l].astype(_BF16), w_pool=w_pool[l].astype(_BF16),
            pool_scale=pool_scale[l][None], w_out=w_out[l].astype(_BF16),
            norm2_g=norm2_g[l][None], w_up=w_up[l].astype(_BF16), ffn_dw=ffn_dw[l],
            w_down=w_down[l].astype(_BF16)))
    fg = final_g[None]

    xm = jnp.broadcast_to(meta_tokens[None], (batch, N_META, d))
    empty = (jnp.zeros((1, batch, CONV_HALO, d), _F32), jnp.zeros((1, batch, POOL_HALO, d), _F32),
             jnp.zeros((1, batch, FFN_HALO, 2 * D_FF), _F32), 0)
    _, meta_states = _trunk(xm, [empty] * depth, layers, fg, nb=batch, tt=N_META, pos0=0,
                            tag="meta")

    seeded = [(c[None], p[None], f[None], 0) for c, p, f in meta_states]
    y_prompt, prompt_states = _trunk(x_prompt, seeded, layers, fg, nb=batch, tt=PROMPT_TT,
                                     pos0=N_META, tag="prompt")

    carried = [(state_conv, state_pool, state_ffn, l) for l in range(depth)]
    y_sample, sample_states = _trunk(x_sample, carried, layers, fg, nb=SAMPLE_NB, tt=dec_seq,
                                     pos0=past_len, tag="sample")

    conv_p, pool_p, ffn_p = (jnp.stack(leaf) for leaf in zip(*prompt_states))
    conv_s, pool_s, ffn_s = (jnp.stack(leaf) for leaf in zip(*sample_states))
    return (y_prompt, y_sample, conv_p, pool_p, ffn_p, conv_s, pool_s, ffn_s)
```

```python
import functools

import jax
import jax.numpy as jnp
from jax import lax
from jax.experimental import pallas as pl
from jax.experimental.pallas import tpu as pltpu

D_MODEL = 1024
CONV_W = 31
POOL_WINDOWS = (2, 4, 8, 16)
POOL_MAX = max(POOL_WINDOWS)
N_POOL_GROUPS = len(POOL_WINDOWS)
POOL_GC = D_MODEL // N_POOL_GROUPS
D_FF = 2816
FFN_CONV_W = 3
N_META = 16
EPS = 1e-6

SUBLANES = 8
LANES = 128
CONV_HALO = CONV_W - 1
POOL_HALO = POOL_MAX - 1
FFN_HALO = FFN_CONV_W - 1
COL_CHUNK = 256
CONV_TIME_BLOCK = 16
VMEM_LIMIT_BYTES = 56 * 1024 * 1024

_F32 = jnp.float32
_BF16 = jnp.bfloat16


def _dot(a, b):
    return jnp.dot(a, b, preferred_element_type=_F32)


def _rms(x, g):
    ms = jnp.mean(x * x, axis=-1, keepdims=True)
    return x * lax.rsqrt(ms + EPS) * g


def _to_rows(a):
    nb, r, c = a.shape
    return jnp.swapaxes(a, 0, 1).reshape(r * nb, c)


def _from_rows(a, nb):
    rows, c = a.shape
    return jnp.swapaxes(a.reshape(rows // nb, nb, c), 0, 1)


def _mixer_kernel(x_ref, cst_ref, pst_ref, g1_ref, win_ref, wdw_ref, cb_ref, lng_ref, lnb_ref,
                  wco_ref, wpool_ref, psc_ref, wout_ref, nconv_all_ref, npool_all_ref,
                  xo_ref, nconv_ref, npool_ref,
                  h_ref, extc_ref, extp_ref, conv_ref, ga_ref, gb_ref,
                  *, nb, tt, nt, pos0, x_natural):
    del nconv_all_ref, npool_all_ref
    d = D_MODEL
    tm = tt * nb
    hc = CONV_HALO * nb
    hp = POOL_HALO * nb
    t_idx = pl.program_id(1)

    @pl.when(t_idx == 0)
    def _():
        extc_ref[0:hc, :] = cst_ref[...].reshape(hc, d)
        extp_ref[0:hp, :] = pst_ref[...].reshape(hp, d)

    x = _to_rows(x_ref[...]) if x_natural else x_ref[...]
    h_ref[...] = _rms(x, g1_ref[...]).astype(_BF16)

    def glu(c):
        lo, hi = c * COL_CHUNK, (c + 1) * COL_CHUNK
        za = _dot(h_ref[...], win_ref[:, lo:hi])
        zg = _dot(h_ref[...], win_ref[:, d + lo:d + hi])
        extc_ref[hc:hc + tm, lo:hi] = za * jax.nn.sigmoid(zg)

    def conv(c):
        for l0 in range(c * COL_CHUNK, (c + 1) * COL_CHUNK, LANES):
            l1 = l0 + LANES
            w = [wdw_ref[k, :, l0:l1] for k in range(CONV_W)]
            bias = cb_ref[:, l0:l1]
            for r_off in range(0, nb, SUBLANES):
                for t0 in range(0, tt, CONV_TIME_BLOCK):
                    n_out = min(CONV_TIME_BLOCK, tt - t0)
                    acc = [None] * n_out
                    for j in range(n_out + CONV_W - 1):
                        r = (t0 + j) * nb + r_off
                        e = extc_ref[r:r + SUBLANES, l0:l1]
                        for s in range(n_out):
                            k = j - s
                            if 0 <= k < CONV_W:
                                term = w[k] * e
                                acc[s] = term if acc[s] is None else acc[s] + term
                    for s in range(n_out):
                        r = (t0 + s) * nb + r_off
                        conv_ref[r:r + SUBLANES, l0:l1] = acc[s] + bias

    n_chunks = d // COL_CHUNK
    glu(0)
    for c in range(n_chunks):
        if c + 1 < n_chunks:
            glu(c + 1)
        else:
            extp_ref[hp:hp + tm, :] = _dot(h_ref[...], win_ref[:, 2 * d:3 * d])
            ga_ref[...] = _dot(h_ref[...], win_ref[:, 3 * d:4 * d])
            gb_ref[...] = _dot(h_ref[...], win_ref[:, 4 * d:5 * d])
        conv(c)

    a = conv_ref[...]
    mu = jnp.mean(a, axis=-1, keepdims=True)
    ac = a - mu
    var = jnp.mean(ac * ac, axis=-1, keepdims=True)
    a = ac * lax.rsqrt(var + EPS) * lng_ref[...] + lnb_ref[...]
    a = a * jax.nn.sigmoid(a)
    a_out = _dot(a.astype(_BF16), wco_ref[...])

    if pos0 + 1 < POOL_MAX:
        row = lax.broadcasted_iota(jnp.int32, (tm, 1), 0)
        pos = pos0 + t_idx * tt + row // nb
    p_parts = []
    for gi, win in enumerate(POOL_WINDOWS):
        l0, l1 = gi * POOL_GC, (gi + 1) * POOL_GC
        cur = extp_ref[hp:hp + tm, l0:l1]
        s = cur
        for j in range(1, win):
            s = s + extp_ref[hp - j * nb:hp - j * nb + tm, l0:l1]
        if pos0 + 1 < POOL_MAX:
            mean = s / jnp.minimum(pos + 1, win).astype(_F32)
        else:
            mean = s * (1.0 / win)
        p = (mean - cur).astype(_BF16)
        p_parts.append(_dot(p, wpool_ref[gi]) * psc_ref[:, l0:l1])
    p_out = jnp.concatenate(p_parts, axis=-1)

    m = jax.nn.sigmoid(ga_ref[...]) * a_out + jax.nn.sigmoid(gb_ref[...]) * p_out
    xo_ref[...] = x + _dot(m.astype(_BF16), wout_ref[...])

    if nt > 1:
        @pl.when(t_idx < nt - 1)
        def _():
            extc_ref[0:hc, :] = extc_ref[tm:tm + hc, :]
            extp_ref[0:hp, :] = extp_ref[tm:tm + hp, :]

    @pl.when(t_idx == nt - 1)
    def _():
        nconv_ref[...] = extc_ref[tm:tm + hc, :].reshape(CONV_HALO, nb, d)
        npool_ref[...] = extp_ref[tm:tm + hp, :].reshape(POOL_HALO, nb, d)


def _ffn_kernel(x_ref, fst_ref, g2_ref, wup_ref, fdw_ref, wdown_ref, fg_ref, nffn_all_ref,
                xo_ref, nffn_ref,
                h_ref, tail_ref, act_ref, *, nb, tt, nt, final_norm, y_natural):
    del nffn_all_ref
    f = D_FF
    tm = tt * nb
    hf = FFN_HALO * nb
    t_idx = pl.program_id(1)

    @pl.when(t_idx == 0)
    def _():
        tail_ref[...] = _to_rows(fst_ref[...])

    x = x_ref[...]
    h_ref[...] = _rms(x, g2_ref[...]).astype(_BF16)

    def conv3(up, lo, hi):
        ext = jnp.concatenate([tail_ref[:, lo:hi], up], axis=0)
        y = fdw_ref[0:1, lo:hi] * ext[0:tm]
        for k in range(1, FFN_CONV_W):
            y = y + fdw_ref[k:k + 1, lo:hi] * ext[k * nb:k * nb + tm]
        tail_ref[:, lo:hi] = ext[tm:tm + hf]
        return y

    for c in range(f // COL_CHUNK):
        lo, hi = c * COL_CHUNK, (c + 1) * COL_CHUNK
        yv = conv3(_dot(h_ref[...], wup_ref[:, lo:hi]), lo, hi)
        yg = conv3(_dot(h_ref[...], wup_ref[:, f + lo:f + hi]), f + lo, f + hi)
        act_ref[:, lo:hi] = (jax.nn.gelu(yg) * yv).astype(_BF16)
    y = x + _dot(act_ref[...], wdown_ref[...])
    if final_norm:
        y = _rms(y, fg_ref[...])
    xo_ref[...] = _from_rows(y, nb) if y_natural else y

    @pl.when(t_idx == nt - 1)
    def _():
        nffn_ref[...] = _from_rows(tail_ref[...], nb)


def _const_spec(shape):
    zeros = (0,) * len(shape)
    return pl.BlockSpec(shape, lambda g, t: zeros, pipeline_mode=pl.Buffered(1))


def _layer_spec(shape, layer):
    zeros = (0,) * (len(shape) - 1)
    return pl.BlockSpec((None,) + tuple(shape[1:]), lambda g, t: (layer,) + zeros,
                        pipeline_mode=pl.Buffered(1))


def _state_tm_spec(layer, nb, steps, cols):
    return pl.BlockSpec((None, steps, nb, cols), lambda g, t: (layer, 0, g, 0))


def _state_seq_spec(layer, nb, steps, cols):
    return pl.BlockSpec((None, nb, steps, cols), lambda g, t: (layer, g, 0, 0))


_ALIASED = pl.BlockSpec(memory_space=pl.ANY)


def _natural_spec(nb, tt, cols):
    return pl.BlockSpec((nb, tt, cols), lambda g, t: (g, t, 0))


def _rows_spec(rows, cols):
    return pl.BlockSpec((None, rows, cols), lambda g, t: (g, t, 0))


def _compiler_params():
    return pltpu.CompilerParams(dimension_semantics=("arbitrary", "arbitrary"),
                                vmem_limit_bytes=VMEM_LIMIT_BYTES)


def _mixer_call(x, cst, pst, st_layer, nconv_all, npool_all, layer, w, *, nb, tt, pos0,
                x_natural, name):
    d = D_MODEL
    seqs = cst.shape[2]
    groups = seqs // nb
    steps = x.shape[1] if x_natural else x.shape[1] // nb
    tm = tt * nb
    nt = steps // tt
    assert seqs == groups * nb and steps == nt * tt
    hc, hp = CONV_HALO * nb, POOL_HALO * nb
    assert nt == 1 or tm >= hc
    kern = functools.partial(_mixer_kernel, nb=nb, tt=tt, nt=nt, pos0=pos0, x_natural=x_natural)
    weights = (w["norm1_g"], w["w_in"], w["conv_dw"], w["conv_b"], w["ln_g"], w["ln_b"],
               w["w_conv_out"], w["w_pool"], w["pool_scale"], w["w_out"])
    x_spec = _natural_spec(nb, tt, d) if x_natural else _rows_spec(tm, d)
    n_in = 3 + len(weights)
    return pl.pallas_call(
        kern,
        grid=(groups, nt),
        in_specs=[x_spec, _state_tm_spec(st_layer, nb, CONV_HALO, d),
                  _state_tm_spec(st_layer, nb, POOL_HALO, d)]
        + [_layer_spec(a.shape, layer) for a in weights] + [_ALIASED, _ALIASED],
        out_specs=[_rows_spec(tm, d), _state_tm_spec(layer, nb, CONV_HALO, d),
                   _state_tm_spec(layer, nb, POOL_HALO, d)],
        out_shape=[jax.ShapeDtypeStruct((groups, steps * nb, d), _F32),
                   jax.ShapeDtypeStruct(nconv_all.shape, _F32),
                   jax.ShapeDtypeStruct(npool_all.shape, _F32)],
        input_output_aliases={n_in: 1, n_in + 1: 2},
        scratch_shapes=[pltpu.VMEM((tm, d), _BF16),
                        pltpu.VMEM((hc + tm, d), _F32),
                        pltpu.VMEM((hp + tm, d), _F32),
                        pltpu.VMEM((tm, d), _F32),
                        pltpu.VMEM((tm, d), _F32),
                        pltpu.VMEM((tm, d), _F32)],
        compiler_params=_compiler_params(),
        name=name,
    )(x, cst, pst, *weights, nconv_all, npool_all)


def _ffn_call(x, fst, st_layer, nffn_all, layer, w, *, nb, tt, final_norm, y_natural, name):
    d = D_MODEL
    groups, rows, _ = x.shape
    seqs = groups * nb
    steps = rows // nb
    tm = tt * nb
    nt = steps // tt
    assert steps == nt * tt
    hf = FFN_HALO * nb
    assert nt == 1 or tm >= hf
    kern = functools.partial(_ffn_kernel, nb=nb, tt=tt, nt=nt, final_norm=final_norm,
                             y_natural=y_natural)
    weights = (w["norm2_g"], w["w_up"], w["ffn_dw"], w["w_down"])
    if y_natural:
        y_spec, y_shape = _natural_spec(nb, tt, d), (seqs, steps, d)
    else:
        y_spec, y_shape = _rows_spec(tm, d), (groups, rows, d)
    n_in = 2 + len(weights) + 1
    return pl.pallas_call(
        kern,
        grid=(groups, nt),
        in_specs=[_rows_spec(tm, d), _state_seq_spec(st_layer, nb, FFN_HALO, 2 * D_FF)]
        + [_layer_spec(a.shape, layer) for a in weights]
        + [_const_spec(w["final_g"].shape), _ALIASED],
        out_specs=[y_spec, _state_seq_spec(layer, nb, FFN_HALO, 2 * D_FF)],
        out_shape=[jax.ShapeDtypeStruct(y_shape, _F32),
                   jax.ShapeDtypeStruct(nffn_all.shape, _F32)],
        input_output_aliases={n_in: 1},
        scratch_shapes=[pltpu.VMEM((tm, d), _BF16),
                        pltpu.VMEM((hf, 2 * D_FF), _F32),
                        pltpu.VMEM((tm, D_FF), _BF16)],
        compiler_params=_compiler_params(),
        name=name,
    )(x, fst, *weights, w["final_g"], nffn_all)


def _trunk(x, states, st_layer, seqs, w, *, nb, tt, pos0, tag):
    depth = w["w_in"].shape[0]
    cst, pst, fst = states
    d = D_MODEL
    nconv = pl.empty((depth, CONV_HALO, seqs, d), _F32)
    npool = pl.empty((depth, POOL_HALO, seqs, d), _F32)
    nffn = pl.empty((depth, seqs, FFN_HALO, 2 * D_FF), _F32)
    for l in range(depth):
        x, nconv, npool = _mixer_call(x, cst, pst, st_layer(l), nconv, npool, l, w, nb=nb, tt=tt,
                                      pos0=pos0, x_natural=(l == 0), name=f"mixer_{tag}_{l}")
        x, nffn = _ffn_call(x, fst, st_layer(l), nffn, l, w, nb=nb, tt=tt,
                            final_norm=(l == depth - 1), y_natural=(l == depth - 1),
                            name=f"ffn_{tag}_{l}")
    return x, (nconv, npool, nffn)


PROMPT_TT = 64
SAMPLE_NB = 32


def kernel(x_prompt, x_sample, state_conv, state_pool, state_ffn, meta_tokens, norm1_g, w_in, conv_dw, conv_b, ln_g, ln_b, w_conv_out, w_pool, pool_scale, w_out, norm2_g, w_up, ffn_dw, w_down, final_g):
    depth = w_in.shape[0]
    batch, seq, d = x_prompt.shape
    dec_batch, dec_seq, _ = x_sample.shape
    past_len = 16384

    row = lambda a: a[:, None, :]
    w = dict(
        norm1_g=row(norm1_g), w_in=w_in.astype(_BF16),
        conv_dw=jnp.broadcast_to(conv_dw[:, :, None, :], (depth, CONV_W, SUBLANES, d)),
        conv_b=row(conv_b), ln_g=row(ln_g), ln_b=row(ln_b),
        w_conv_out=w_conv_out.astype(_BF16), w_pool=w_pool.astype(_BF16),
        pool_scale=row(pool_scale), w_out=w_out.astype(_BF16),
        norm2_g=row(norm2_g), w_up=w_up.astype(_BF16), ffn_dw=ffn_dw,
        w_down=w_down.astype(_BF16), final_g=final_g[None])
    time_major = lambda a: jnp.transpose(a, (0, 2, 1, 3))

    xm = jnp.broadcast_to(meta_tokens[None], (batch, N_META, d))
    empty = (jnp.zeros((1, CONV_HALO, batch, d), _F32), jnp.zeros((1, POOL_HALO, batch, d), _F32),
             jnp.zeros((1, batch, FFN_HALO, 2 * D_FF), _F32))
    _, meta_states = _trunk(xm, empty, lambda l: 0, batch, w, nb=batch, tt=N_META, pos0=0,
                            tag="meta")

    y_prompt, (conv_p, pool_p, ffn_p) = _trunk(x_prompt, meta_states, lambda l: l, batch, w,
                                               nb=batch, tt=PROMPT_TT, pos0=N_META, tag="prompt")

    carried = (time_major(state_conv), time_major(state_pool), state_ffn)
    y_sample, (conv_s, pool_s, ffn_s) = _trunk(x_sample, carried, lambda l: l, dec_batch, w,
                                               nb=SAMPLE_NB, tt=dec_seq, pos0=past_len,
                                               tag="sample")

    return (y_prompt, y_sample, time_major(conv_p), time_major(pool_p), ffn_p,
            time_major(conv_s), time_major(pool_s), ffn_s)
```

```python
import functools

import jax
import jax.numpy as jnp
from jax import lax
from jax.experimental import pallas as pl
from jax.experimental.pallas import tpu as pltpu

D_MODEL = 1024
CONV_W = 31
POOL_WINDOWS = (2, 4, 8, 16)
POOL_MAX = max(POOL_WINDOWS)
N_POOL_GROUPS = len(POOL_WINDOWS)
POOL_GC = D_MODEL // N_POOL_GROUPS
D_FF = 2816
FFN_CONV_W = 3
N_META = 16
EPS = 1e-6

SUBLANES = 8
LANES = 128
CONV_HALO = CONV_W - 1
POOL_HALO = POOL_MAX - 1
FFN_HALO = FFN_CONV_W - 1
COL_CHUNK = 256
CONV_TIME_BLOCK = 16
VMEM_LIMIT_BYTES = 56 * 1024 * 1024

_F32 = jnp.float32
_BF16 = jnp.bfloat16


def _dot(a, b):
    return jnp.dot(a, b, preferred_element_type=_F32)


def _rms(x, g):
    ms = jnp.mean(x * x, axis=-1, keepdims=True)
    return x * lax.rsqrt(ms + EPS) * g


def _to_rows(a):
    nb, r, c = a.shape
    return jnp.swapaxes(a, 0, 1).reshape(r * nb, c)


def _from_rows(a, nb):
    rows, c = a.shape
    return jnp.swapaxes(a.reshape(rows // nb, nb, c), 0, 1)


def _mixer_kernel(x_ref, cst_ref, pst_ref, zero_ref, g1_ref, win_ref, wdw_ref, cb_ref, lng_ref, lnb_ref,
                  wco_ref, wpool_ref, psc_ref, wout_ref, nconv_all_ref, npool_all_ref,
                  xo_ref, nconv_ref, npool_ref,
                  h_ref, extc_ref, extp_ref, conv_ref, ga_ref, gb_ref,
                  *, nb, tt, nt, pos0, x_natural):
    del nconv_all_ref, npool_all_ref
    d = D_MODEL
    tm = tt * nb
    hc = CONV_HALO * nb
    hp = POOL_HALO * nb
    t_idx = pl.program_id(1)

    @pl.when(t_idx == 0)
    def _():
        extc_ref[0:hc, :] = cst_ref[...].reshape(hc, d)
        extp_ref[0:hp, :] = pst_ref[...].reshape(hp, d)

    x = _to_rows(x_ref[...]) if x_natural else x_ref[...]
    h_ref[...] = _rms(x, g1_ref[...]).astype(_BF16)

    n_chunks = d // COL_CHUNK
    side = ([(extp_ref, hp, 2 * d + i * COL_CHUNK) for i in range(n_chunks)]
            + [(ga_ref, 0, 3 * d + i * COL_CHUNK) for i in range(n_chunks)]
            + [(gb_ref, 0, 4 * d + i * COL_CHUNK) for i in range(n_chunks)])
    per_chunk = len(side) // (n_chunks - 1)
    bits_of = lambda v: lax.bitcast_convert_type(v, jnp.uint32)

    def side_dots(i):
        tok = None
        for dst, row0, wcol in side[i * per_chunk:(i + 1) * per_chunk]:
            r = _dot(h_ref[...], win_ref[:, wcol:wcol + COL_CHUNK])
            dst[row0:row0 + tm, wcol % d:wcol % d + COL_CHUNK] = r
            tok = bits_of(r) if tok is None else tok | bits_of(r)
        return tok

    def anchored(y, tok):
        return lax.bitcast_convert_type(bits_of(y) ^ (tok & zero_ref[...]), _F32)

    def glu(c):
        lo, hi = c * COL_CHUNK, (c + 1) * COL_CHUNK
        za = _dot(h_ref[...], win_ref[:, lo:hi])
        zg = _dot(h_ref[...], win_ref[:, d + lo:d + hi])
        y = za * jax.nn.sigmoid(zg)
        if c > 0:
            y = anchored(y, side_dots(c - 1))
        extc_ref[hc:hc + tm, lo:hi] = y

    def conv(c):
        for l0 in range(c * COL_CHUNK, (c + 1) * COL_CHUNK, LANES):
            l1 = l0 + LANES
            w = [wdw_ref[k, :, l0:l1] for k in range(CONV_W)]
            bias = cb_ref[:, l0:l1]
            for r_off in range(0, nb, SUBLANES):
                for t0 in range(0, tt, CONV_TIME_BLOCK):
                    n_out = min(CONV_TIME_BLOCK, tt - t0)
                    acc = [None] * n_out
                    for j in range(n_out + CONV_W - 1):
                        r = (t0 + j) * nb + r_off
                        e = extc_ref[r:r + SUBLANES, l0:l1]
                        for s in range(n_out):
                            k = j - s
                            if 0 <= k < CONV_W:
                                term = w[k] * e
                                acc[s] = term if acc[s] is None else acc[s] + term
                    for s in range(n_out):
                        r = (t0 + s) * nb + r_off
                        conv_ref[r:r + SUBLANES, l0:l1] = acc[s] + bias

    glu(0)
    for c in range(n_chunks):
        if c + 1 < n_chunks:
            glu(c + 1)
        conv(c)

    a = conv_ref[...]
    mu = jnp.mean(a, axis=-1, keepdims=True)
    ac = a - mu
    var = jnp.mean(ac * ac, axis=-1, keepdims=True)
    a = ac * lax.rsqrt(var + EPS) * lng_ref[...] + lnb_ref[...]
    a = a * jax.nn.sigmoid(a)
    a_out = _dot(a.astype(_BF16), wco_ref[...])

    if pos0 + 1 < POOL_MAX:
        row = lax.broadcasted_iota(jnp.int32, (tm, 1), 0)
        pos = pos0 + t_idx * tt + row // nb
    p_parts = []
    for gi, win in enumerate(POOL_WINDOWS):
        l0, l1 = gi * POOL_GC, (gi + 1) * POOL_GC
        cur = extp_ref[hp:hp + tm, l0:l1]
        s = cur
        for j in range(1, win):
            s = s + extp_ref[hp - j * nb:hp - j * nb + tm, l0:l1]
        if pos0 + 1 < POOL_MAX:
            mean = s / jnp.minimum(pos + 1, win).astype(_F32)
        else:
            mean = s * (1.0 / win)
        p = (mean - cur).astype(_BF16)
        p_parts.append(_dot(p, wpool_ref[gi]) * psc_ref[:, l0:l1])
    p_out = jnp.concatenate(p_parts, axis=-1)

    m = jax.nn.sigmoid(ga_ref[...]) * a_out + jax.nn.sigmoid(gb_ref[...]) * p_out
    xo_ref[...] = x + _dot(m.astype(_BF16), wout_ref[...])

    if nt > 1:
        @pl.when(t_idx < nt - 1)
        def _():
            extc_ref[0:hc, :] = extc_ref[tm:tm + hc, :]
            extp_ref[0:hp, :] = extp_ref[tm:tm + hp, :]

    @pl.when(t_idx == nt - 1)
    def _():
        nconv_ref[...] = extc_ref[tm:tm + hc, :].reshape(CONV_HALO, nb, d)
        npool_ref[...] = extp_ref[tm:tm + hp, :].reshape(POOL_HALO, nb, d)


def _ffn_kernel(x_ref, fst_ref, g2_ref, wup_ref, fdw_ref, wdown_ref, fg_ref, nffn_all_ref,
                xo_ref, nffn_ref,
                h_ref, tail_ref, act_ref, *, nb, tt, nt, final_norm, y_natural):
    del nffn_all_ref
    f = D_FF
    tm = tt * nb
    hf = FFN_HALO * nb
    t_idx = pl.program_id(1)

    @pl.when(t_idx == 0)
    def _():
        tail_ref[...] = _to_rows(fst_ref[...])

    x = x_ref[...]
    h_ref[...] = _rms(x, g2_ref[...]).astype(_BF16)

    def conv3(up, lo, hi):
        ext = jnp.concatenate([tail_ref[:, lo:hi], up], axis=0)
        y = fdw_ref[0:1, lo:hi] * ext[0:tm]
        for k in range(1, FFN_CONV_W):
            y = y + fdw_ref[k:k + 1, lo:hi] * ext[k * nb:k * nb + tm]
        tail_ref[:, lo:hi] = ext[tm:tm + hf]
        return y

    for c in range(f // COL_CHUNK):
        lo, hi = c * COL_CHUNK, (c + 1) * COL_CHUNK
        yv = conv3(_dot(h_ref[...], wup_ref[:, lo:hi]), lo, hi)
        yg = conv3(_dot(h_ref[...], wup_ref[:, f + lo:f + hi]), f + lo, f + hi)
        act_ref[:, lo:hi] = (jax.nn.gelu(yg) * yv).astype(_BF16)
    y = x + _dot(act_ref[...], wdown_ref[...])
    if final_norm:
        y = _rms(y, fg_ref[...])
    xo_ref[...] = _from_rows(y, nb) if y_natural else y

    @pl.when(t_idx == nt - 1)
    def _():
        nffn_ref[...] = _from_rows(tail_ref[...], nb)


def _const_spec(shape):
    zeros = (0,) * len(shape)
    return pl.BlockSpec(shape, lambda g, t: zeros, pipeline_mode=pl.Buffered(1))


def _layer_spec(shape, layer):
    zeros = (0,) * (len(shape) - 1)
    return pl.BlockSpec((None,) + tuple(shape[1:]), lambda g, t: (layer,) + zeros,
                        pipeline_mode=pl.Buffered(1))


def _state_tm_spec(layer, nb, steps, cols):
    return pl.BlockSpec((None, steps, nb, cols), lambda g, t: (layer, 0, g, 0))


def _state_seq_spec(layer, nb, steps, cols):
    return pl.BlockSpec((None, nb, steps, cols), lambda g, t: (layer, g, 0, 0))


_ALIASED = pl.BlockSpec(memory_space=pl.ANY)


def _natural_spec(nb, tt, cols):
    return pl.BlockSpec((nb, tt, cols), lambda g, t: (g, t, 0))


def _rows_spec(rows, cols):
    return pl.BlockSpec((None, rows, cols), lambda g, t: (g, t, 0))


def _compiler_params():
    return pltpu.CompilerParams(dimension_semantics=("arbitrary", "arbitrary"),
                                vmem_limit_bytes=VMEM_LIMIT_BYTES)


def _mixer_call(x, cst, pst, st_layer, nconv_all, npool_all, layer, w, *, nb, tt, pos0,
                x_natural, name):
    d = D_MODEL
    seqs = cst.shape[2]
    groups = seqs // nb
    steps = x.shape[1] if x_natural else x.shape[1] // nb
    tm = tt * nb
    nt = steps // tt
    assert seqs == groups * nb and steps == nt * tt
    hc, hp = CONV_HALO * nb, POOL_HALO * nb
    assert nt == 1 or tm >= hc
    kern = functools.partial(_mixer_kernel, nb=nb, tt=tt, nt=nt, pos0=pos0, x_natural=x_natural)
    weights = (w["norm1_g"], w["w_in"], w["conv_dw"], w["conv_b"], w["ln_g"], w["ln_b"],
               w["w_conv_out"], w["w_pool"], w["pool_scale"], w["w_out"])
    x_spec = _natural_spec(nb, tt, d) if x_natural else _rows_spec(tm, d)
    zero_bits = jnp.zeros((tm, COL_CHUNK), jnp.uint32)
    n_in = 4 + len(weights)
    return pl.pallas_call(
        kern,
        grid=(groups, nt),
        in_specs=[x_spec, _state_tm_spec(st_layer, nb, CONV_HALO, d),
                  _state_tm_spec(st_layer, nb, POOL_HALO, d), _const_spec(zero_bits.shape)]
        + [_layer_spec(a.shape, layer) for a in weights] + [_ALIASED, _ALIASED],
        out_specs=[_rows_spec(tm, d), _state_tm_spec(layer, nb, CONV_HALO, d),
                   _state_tm_spec(layer, nb, POOL_HALO, d)],
        out_shape=[jax.ShapeDtypeStruct((groups, steps * nb, d), _F32),
                   jax.ShapeDtypeStruct(nconv_all.shape, _F32),
                   jax.ShapeDtypeStruct(npool_all.shape, _F32)],
        input_output_aliases={n_in: 1, n_in + 1: 2},
        scratch_shapes=[pltpu.VMEM((tm, d), _BF16),
                        pltpu.VMEM((hc + tm, d), _F32),
                        pltpu.VMEM((hp + tm, d), _F32),
                        pltpu.VMEM((tm, d), _F32),
                        pltpu.VMEM((tm, d), _F32),
                        pltpu.VMEM((tm, d), _F32)],
        compiler_params=_compiler_params(),
        name=name,
    )(x, cst, pst, zero_bits, *weights, nconv_all, npool_all)


def _ffn_call(x, fst, st_layer, nffn_all, layer, w, *, nb, tt, final_norm, y_natural, name):
    d = D_MODEL
    groups, rows, _ = x.shape
    seqs = groups * nb
    steps = rows // nb
    tm = tt * nb
    nt = steps // tt
    assert steps == nt * tt
    hf = FFN_HALO * nb
    assert nt == 1 or tm >= hf
    kern = functools.partial(_ffn_kernel, nb=nb, tt=tt, nt=nt, final_norm=final_norm,
                             y_natural=y_natural)
    weights = (w["norm2_g"], w["w_up"], w["ffn_dw"], w["w_down"])
    if y_natural:
        y_spec, y_shape = _natural_spec(nb, tt, d), (seqs, steps, d)
    else:
        y_spec, y_shape = _rows_spec(tm, d), (groups, rows, d)
    n_in = 2 + len(weights) + 1
    return pl.pallas_call(
        kern,
        grid=(groups, nt),
        in_specs=[_rows_spec(tm, d), _state_seq_spec(st_layer, nb, FFN_HALO, 2 * D_FF)]
        + [_layer_spec(a.shape, layer) for a in weights]
        + [_const_spec(w["final_g"].shape), _ALIASED],
        out_specs=[y_spec, _state_seq_spec(layer, nb, FFN_HALO, 2 * D_FF)],
        out_shape=[jax.ShapeDtypeStruct(y_shape, _F32),
                   jax.ShapeDtypeStruct(nffn_all.shape, _F32)],
        input_output_aliases={n_in: 1},
        scratch_shapes=[pltpu.VMEM((tm, d), _BF16),
                        pltpu.VMEM((hf, 2 * D_FF), _F32),
                        pltpu.VMEM((tm, D_FF), _BF16)],
        compiler_params=_compiler_params(),
        name=name,
    )(x, fst, *weights, w["final_g"], nffn_all)


def _trunk(x, states, st_layer, seqs, w, *, nb, tt, pos0, tag):
    depth = w["w_in"].shape[0]
    cst, pst, fst = states
    d = D_MODEL
    nconv = pl.empty((depth, CONV_HALO, seqs, d), _F32)
    npool = pl.empty((depth, POOL_HALO, seqs, d), _F32)
    nffn = pl.empty((depth, seqs, FFN_HALO, 2 * D_FF), _F32)
    for l in range(depth):
        x, nconv, npool = _mixer_call(x, cst, pst, st_layer(l), nconv, npool, l, w, nb=nb, tt=tt,
                                      pos0=pos0, x_natural=(l == 0), name=f"mixer_{tag}_{l}")
        x, nffn = _ffn_call(x, fst, st_layer(l), nffn, l, w, nb=nb, tt=tt,
                            final_norm=(l == depth - 1), y_natural=(l == depth - 1),
                            name=f"ffn_{tag}_{l}")
    return x, (nconv, npool, nffn)


PROMPT_TT = 64
SAMPLE_NB = 32


def kernel(x_prompt, x_sample, state_conv, state_pool, state_ffn, meta_tokens, norm1_g, w_in, conv_dw, conv_b, ln_g, ln_b, w_conv_out, w_pool, pool_scale, w_out, norm2_g, w_up, ffn_dw, w_down, final_g):
    depth = w_in.shape[0]
    batch, seq, d = x_prompt.shape
    dec_batch, dec_seq, _ = x_sample.shape
    past_len = 16384

    row = lambda a: a[:, None, :]
    w = dict(
        norm1_g=row(norm1_g), w_in=w_in.astype(_BF16),
        conv_dw=jnp.broadcast_to(conv_dw[:, :, None, :], (depth, CONV_W, SUBLANES, d)),
        conv_b=row(conv_b), ln_g=row(ln_g), ln_b=row(ln_b),
        w_conv_out=w_conv_out.astype(_BF16), w_pool=w_pool.astype(_BF16),
        pool_scale=row(pool_scale), w_out=w_out.astype(_BF16),
        norm2_g=row(norm2_g), w_up=w_up.astype(_BF16), ffn_dw=ffn_dw,
        w_down=w_down.astype(_BF16), final_g=final_g[None])
    time_major = lambda a: jnp.transpose(a, (0, 2, 1, 3))

    xm = jnp.broadcast_to(meta_tokens[None], (batch, N_META, d))
    empty = (jnp.zeros((1, CONV_HALO, batch, d), _F32), jnp.zeros((1, POOL_HALO, batch, d), _F32),
             jnp.zeros((1, batch, FFN_HALO, 2 * D_FF), _F32))
    _, meta_states = _trunk(xm, empty, lambda l: 0, batch, w, nb=batch, tt=N_META, pos0=0,
                            tag="meta")

    y_prompt, (conv_p, pool_p, ffn_p) = _trunk(x_prompt, meta_states, lambda l: l, batch, w,
                                               nb=batch, tt=PROMPT_TT, pos0=N_META, tag="prompt")

    carried = (time_major(state_conv), time_major(state_pool), state_ffn)
    y_sample, (conv_s, pool_s, ffn_s) = _trunk(x_sample, carried, lambda l: l, dec_batch, w,
                                               nb=SAMPLE_NB, tt=dec_seq, pos0=past_len,
                                               tag="sample")

    return (y_prompt, y_sample, time_major(conv_p), time_major(pool_p), ffn_p,
            time_major(conv_s), time_major(pool_s), ffn_s)
```

```python
import functools

import jax
import jax.numpy as jnp
from jax import lax
from jax.experimental import pallas as pl
from jax.experimental.pallas import tpu as pltpu

D_MODEL = 1024
CONV_W = 31
POOL_WINDOWS = (2, 4, 8, 16)
POOL_MAX = max(POOL_WINDOWS)
N_POOL_GROUPS = len(POOL_WINDOWS)
POOL_GC = D_MODEL // N_POOL_GROUPS
D_FF = 2816
FFN_CONV_W = 3
N_META = 16
EPS = 1e-6

SUBLANES = 8
LANES = 128
CONV_HALO = CONV_W - 1
POOL_HALO = POOL_MAX - 1
FFN_HALO = FFN_CONV_W - 1
COL_CHUNK = 256
CONV_TIME_BLOCK = 16
VMEM_LIMIT_BYTES = 56 * 1024 * 1024

_F32 = jnp.float32
_BF16 = jnp.bfloat16


def _dot(a, b):
    return jnp.dot(a, b, preferred_element_type=_F32)


def _rms(x, g):
    ms = jnp.mean(x * x, axis=-1, keepdims=True)
    return x * lax.rsqrt(ms + EPS) * g


def _to_rows(a):
    nb, r, c = a.shape
    return jnp.swapaxes(a, 0, 1).reshape(r * nb, c)


def _from_rows(a, nb):
    rows, c = a.shape
    return jnp.swapaxes(a.reshape(rows // nb, nb, c), 0, 1)


def _mixer_kernel(x_ref, cst_ref, pst_ref, zero_ref, g1_ref, win_ref, wdw_ref, cb_ref, lng_ref, lnb_ref,
                  wco_ref, wpool_ref, psc_ref, wout_ref, nconv_all_ref, npool_all_ref,
                  xo_ref, nconv_ref, npool_ref,
                  h_ref, extc_ref, extp_ref, conv_ref, ga_ref, gb_ref,
                  *, nb, tt, nt, pos0, x_natural):
    del nconv_all_ref, npool_all_ref
    d = D_MODEL
    tm = tt * nb
    hc = CONV_HALO * nb
    hp = POOL_HALO * nb
    t_idx = pl.program_id(1)

    @pl.when(t_idx == 0)
    def _():
        extc_ref[0:hc, :] = cst_ref[...].reshape(hc, d)
        extp_ref[0:hp, :] = pst_ref[...].reshape(hp, d)

    x = _to_rows(x_ref[...]) if x_natural else x_ref[...]
    h_ref[...] = _rms(x, g1_ref[...]).astype(_BF16)

    n_chunks = d // COL_CHUNK
    side = ([(extp_ref, hp, 2 * d + i * COL_CHUNK) for i in range(n_chunks)]
            + [(ga_ref, 0, 3 * d + i * COL_CHUNK) for i in range(n_chunks)]
            + [(gb_ref, 0, 4 * d + i * COL_CHUNK) for i in range(n_chunks)])
    per_chunk = len(side) // (n_chunks - 1)
    bits_of = lambda v: lax.bitcast_convert_type(v, jnp.uint32)

    def side_dots(i):
        tok = None
        for dst, row0, wcol in side[i * per_chunk:(i + 1) * per_chunk]:
            r = _dot(h_ref[...], win_ref[:, wcol:wcol + COL_CHUNK])
            dst[row0:row0 + tm, wcol % d:wcol % d + COL_CHUNK] = r
            tok = bits_of(r) if tok is None else tok | bits_of(r)
        return tok

    def anchored(y, tok):
        return lax.bitcast_convert_type(bits_of(y) ^ (tok & zero_ref[...]), _F32)

    def glu(c):
        lo, hi = c * COL_CHUNK, (c + 1) * COL_CHUNK
        za = _dot(h_ref[...], win_ref[:, lo:hi])
        zg = _dot(h_ref[...], win_ref[:, d + lo:d + hi])
        y = za * jax.nn.sigmoid(zg)
        if c > 0:
            y = anchored(y, side_dots(c - 1))
        extc_ref[hc:hc + tm, lo:hi] = y

    def conv(c):
        for l0 in range(c * COL_CHUNK, (c + 1) * COL_CHUNK, LANES):
            l1 = l0 + LANES
            w = [wdw_ref[k, :, l0:l1] for k in range(CONV_W)]
            bias = cb_ref[:, l0:l1]
            for r_off in range(0, nb, SUBLANES):
                for t0 in range(0, tt, CONV_TIME_BLOCK):
                    n_out = min(CONV_TIME_BLOCK, tt - t0)
                    acc = [None] * n_out
                    for j in range(n_out + CONV_W - 1):
                        r = (t0 + j) * nb + r_off
                        e = extc_ref[r:r + SUBLANES, l0:l1]
                        for s in range(n_out):
                            k = j - s
                            if 0 <= k < CONV_W:
                                term = w[k] * e
                                acc[s] = term if acc[s] is None else acc[s] + term
                    for s in range(n_out):
                        r = (t0 + s) * nb + r_off
                        conv_ref[r:r + SUBLANES, l0:l1] = acc[s] + bias

    glu(0)
    for c in range(n_chunks):
        if c + 1 < n_chunks:
            glu(c + 1)
        conv(c)

    a = conv_ref[...]
    mu = jnp.mean(a, axis=-1, keepdims=True)
    ac = a - mu
    var = jnp.mean(ac * ac, axis=-1, keepdims=True)
    a = ac * lax.rsqrt(var + EPS) * lng_ref[...] + lnb_ref[...]
    a = a * jax.nn.sigmoid(a)
    a_out = _dot(a.astype(_BF16), wco_ref[...])

    if pos0 + 1 < POOL_MAX:
        row = lax.broadcasted_iota(jnp.int32, (tm, 1), 0)
        pos = pos0 + t_idx * tt + row // nb
    p_parts = []
    for gi, win in enumerate(POOL_WINDOWS):
        l0, l1 = gi * POOL_GC, (gi + 1) * POOL_GC
        cur = extp_ref[hp:hp + tm, l0:l1]
        s = cur
        for j in range(1, win):
            s = s + extp_ref[hp - j * nb:hp - j * nb + tm, l0:l1]
        if pos0 + 1 < POOL_MAX:
            mean = s / jnp.minimum(pos + 1, win).astype(_F32)
        else:
            mean = s * (1.0 / win)
        p = (mean - cur).astype(_BF16)
        p_parts.append(_dot(p, wpool_ref[gi]) * psc_ref[:, l0:l1])
    p_out = jnp.concatenate(p_parts, axis=-1)

    m = jax.nn.sigmoid(ga_ref[...]) * a_out + jax.nn.sigmoid(gb_ref[...]) * p_out
    xo_ref[...] = x + _dot(m.astype(_BF16), wout_ref[...])

    if nt > 1:
        @pl.when(t_idx < nt - 1)
        def _():
            extc_ref[0:hc, :] = extc_ref[tm:tm + hc, :]
            extp_ref[0:hp, :] = extp_ref[tm:tm + hp, :]

    @pl.when(t_idx == nt - 1)
    def _():
        nconv_ref[...] = extc_ref[tm:tm + hc, :].reshape(CONV_HALO, nb, d)
        npool_ref[...] = extp_ref[tm:tm + hp, :].reshape(POOL_HALO, nb, d)


def _ffn_kernel(x_ref, fst_ref, g2_ref, wup_ref, fdw_ref, wdown_ref, fg_ref, nffn_all_ref,
                xo_ref, nffn_ref,
                h_ref, tail_ref, act_ref, *, nb, tt, nt, final_norm, y_natural):
    del nffn_all_ref
    f = D_FF
    tm = tt * nb
    hf = FFN_HALO * nb
    t_idx = pl.program_id(1)

    @pl.when(t_idx == 0)
    def _():
        tail_ref[...] = _to_rows(fst_ref[...])

    x = x_ref[...]
    h_ref[...] = _rms(x, g2_ref[...]).astype(_BF16)

    def conv3(up, lo, hi):
        ext = jnp.concatenate([tail_ref[:, lo:hi], up], axis=0)
        y = fdw_ref[0:1, lo:hi] * ext[0:tm]
        for k in range(1, FFN_CONV_W):
            y = y + fdw_ref[k:k + 1, lo:hi] * ext[k * nb:k * nb + tm]
        tail_ref[:, lo:hi] = ext[tm:tm + hf]
        return y

    for c in range(f // COL_CHUNK):
        lo, hi = c * COL_CHUNK, (c + 1) * COL_CHUNK
        yv = conv3(_dot(h_ref[...], wup_ref[:, lo:hi]), lo, hi)
        yg = conv3(_dot(h_ref[...], wup_ref[:, f + lo:f + hi]), f + lo, f + hi)
        act_ref[:, lo:hi] = (jax.nn.gelu(yg) * yv).astype(_BF16)
    y = x + _dot(act_ref[...], wdown_ref[...])
    if final_norm:
        y = _rms(y, fg_ref[...])
    xo_ref[...] = _from_rows(y, nb) if y_natural else y

    @pl.when(t_idx == nt - 1)
    def _():
        nffn_ref[...] = _from_rows(tail_ref[...], nb)


def _const_spec(shape):
    zeros = (0,) * len(shape)
    return pl.BlockSpec(shape, lambda g, t: zeros, pipeline_mode=pl.Buffered(1))


def _layer_spec(shape, layer):
    zeros = (0,) * (len(shape) - 1)
    return pl.BlockSpec((None,) + tuple(shape[1:]), lambda g, t: (layer,) + zeros,
                        pipeline_mode=pl.Buffered(1))


def _state_tm_spec(layer, nb, steps, cols):
    return pl.BlockSpec((None, steps, nb, cols), lambda g, t: (layer, 0, g, 0))


def _state_seq_spec(layer, nb, steps, cols):
    return pl.BlockSpec((None, nb, steps, cols), lambda g, t: (layer, g, 0, 0))


_ALIASED = pl.BlockSpec(memory_space=pl.ANY)


def _natural_spec(nb, tt, cols):
    return pl.BlockSpec((nb, tt, cols), lambda g, t: (g, t, 0))


def _rows_spec(rows, cols):
    return pl.BlockSpec((None, rows, cols), lambda g, t: (g, t, 0))


def _compiler_params():
    return pltpu.CompilerParams(dimension_semantics=("arbitrary", "arbitrary"),
                                vmem_limit_bytes=VMEM_LIMIT_BYTES)


def _mixer_call(x, cst, pst, st_layer, nconv_all, npool_all, layer, w, *, nb, tt, pos0,
                x_natural, name):
    d = D_MODEL
    seqs = cst.shape[2]
    groups = seqs // nb
    steps = x.shape[1] if x_natural else x.shape[1] // nb
    tm = tt * nb
    nt = steps // tt
    assert seqs == groups * nb and steps == nt * tt
    hc, hp = CONV_HALO * nb, POOL_HALO * nb
    assert nt == 1 or tm >= hc
    kern = functools.partial(_mixer_kernel, nb=nb, tt=tt, nt=nt, pos0=pos0, x_natural=x_natural)
    weights = (w["norm1_g"], w["w_in"], w["conv_dw"], w["conv_b"], w["ln_g"], w["ln_b"],
               w["w_conv_out"], w["w_pool"], w["pool_scale"], w["w_out"])
    x_spec = _natural_spec(nb, tt, d) if x_natural else _rows_spec(tm, d)
    zero_bits = jnp.zeros((tm, COL_CHUNK), jnp.uint32)
    n_in = 4 + len(weights)
    return pl.pallas_call(
        kern,
        grid=(groups, nt),
        in_specs=[x_spec, _state_tm_spec(st_layer, nb, CONV_HALO, d),
                  _state_tm_spec(st_layer, nb, POOL_HALO, d), _const_spec(zero_bits.shape)]
        + [_layer_spec(a.shape, layer) for a in weights] + [_ALIASED, _ALIASED],
        out_specs=[_rows_spec(tm, d), _state_tm_spec(layer, nb, CONV_HALO, d),
                   _state_tm_spec(layer, nb, POOL_HALO, d)],
        out_shape=[jax.ShapeDtypeStruct((groups, steps * nb, d), _F32),
                   jax.ShapeDtypeStruct(nconv_all.shape, _F32),
                   jax.ShapeDtypeStruct(npool_all.shape, _F32)],
        input_output_aliases={n_in: 1, n_in + 1: 2},
        scratch_shapes=[pltpu.VMEM((tm, d), _BF16),
                        pltpu.VMEM((hc + tm, d), _F32),
                        pltpu.VMEM((hp + tm, d), _F32),
                        pltpu.VMEM((tm, d), _F32),
                        pltpu.VMEM((tm, d), _F32),
                        pltpu.VMEM((tm, d), _F32)],
        compiler_params=_compiler_params(),
        name=name,
    )(x, cst, pst, zero_bits, *weights, nconv_all, npool_all)


def _ffn_call(x, fst, st_layer, nffn_all, layer, w, *, nb, tt, final_norm, y_natural, name):
    d = D_MODEL
    groups, rows, _ = x.shape
    seqs = groups * nb
    steps = rows // nb
    tm = tt * nb
    nt = steps // tt
    assert steps == nt * tt
    hf = FFN_HALO * nb
    assert nt == 1 or tm >= hf
    kern = functools.partial(_ffn_kernel, nb=nb, tt=tt, nt=nt, final_norm=final_norm,
                             y_natural=y_natural)
    weights = (w["norm2_g"], w["w_up"], w["ffn_dw"], w["w_down"])
    if y_natural:
        y_spec, y_shape = _natural_spec(nb, tt, d), (seqs, steps, d)
    else:
        y_spec, y_shape = _rows_spec(tm, d), (groups, rows, d)
    n_in = 2 + len(weights) + 1
    return pl.pallas_call(
        kern,
        grid=(groups, nt),
        in_specs=[_rows_spec(tm, d), _state_seq_spec(st_layer, nb, FFN_HALO, 2 * D_FF)]
        + [_layer_spec(a.shape, layer) for a in weights]
        + [_const_spec(w["final_g"].shape), _ALIASED],
        out_specs=[y_spec, _state_seq_spec(layer, nb, FFN_HALO, 2 * D_FF)],
        out_shape=[jax.ShapeDtypeStruct(y_shape, _F32),
                   jax.ShapeDtypeStruct(nffn_all.shape, _F32)],
        input_output_aliases={n_in: 1},
        scratch_shapes=[pltpu.VMEM((tm, d), _BF16),
                        pltpu.VMEM((hf, 2 * D_FF), _F32),
                        pltpu.VMEM((tm, D_FF), _BF16)],
        compiler_params=_compiler_params(),
        name=name,
    )(x, fst, *weights, w["final_g"], nffn_all)


def _trunk(x, states, st_layer, seqs, w, *, nb, tt, ffn_tt, pos0, tag):
    depth = w["w_in"].shape[0]
    cst, pst, fst = states
    d = D_MODEL
    nconv = pl.empty((depth, CONV_HALO, seqs, d), _F32)
    npool = pl.empty((depth, POOL_HALO, seqs, d), _F32)
    nffn = pl.empty((depth, seqs, FFN_HALO, 2 * D_FF), _F32)
    for l in range(depth):
        x, nconv, npool = _mixer_call(x, cst, pst, st_layer(l), nconv, npool, l, w, nb=nb, tt=tt,
                                      pos0=pos0, x_natural=(l == 0), name=f"mixer_{tag}_{l}")
        x, nffn = _ffn_call(x, fst, st_layer(l), nffn, l, w, nb=nb, tt=ffn_tt,
                            final_norm=(l == depth - 1), y_natural=(l == depth - 1),
                            name=f"ffn_{tag}_{l}")
    return x, (nconv, npool, nffn)


PROMPT_TT = 64
PROMPT_FFN_TT = 128
SAMPLE_NB = 32


def kernel(x_prompt, x_sample, state_conv, state_pool, state_ffn, meta_tokens, norm1_g, w_in, conv_dw, conv_b, ln_g, ln_b, w_conv_out, w_pool, pool_scale, w_out, norm2_g, w_up, ffn_dw, w_down, final_g):
    depth = w_in.shape[0]
    batch, seq, d = x_prompt.shape
    dec_batch, dec_seq, _ = x_sample.shape
    past_len = 16384

    row = lambda a: a[:, None, :]
    w = dict(
        norm1_g=row(norm1_g), w_in=w_in.astype(_BF16),
        conv_dw=jnp.broadcast_to(conv_dw[:, :, None, :], (depth, CONV_W, SUBLANES, d)),
        conv_b=row(conv_b), ln_g=row(ln_g), ln_b=row(ln_b),
        w_conv_out=w_conv_out.astype(_BF16), w_pool=w_pool.astype(_BF16),
        pool_scale=row(pool_scale), w_out=w_out.astype(_BF16),
        norm2_g=row(norm2_g), w_up=w_up.astype(_BF16), ffn_dw=ffn_dw,
        w_down=w_down.astype(_BF16), final_g=final_g[None])
    time_major = lambda a: jnp.transpose(a, (0, 2, 1, 3))

    xm = jnp.broadcast_to(meta_tokens[None], (batch, N_META, d))
    empty = (jnp.zeros((1, CONV_HALO, batch, d), _F32), jnp.zeros((1, POOL_HALO, batch, d), _F32),
             jnp.zeros((1, batch, FFN_HALO, 2 * D_FF), _F32))
    _, meta_states = _trunk(xm, empty, lambda l: 0, batch, w, nb=batch, tt=N_META,
                            ffn_tt=N_META, pos0=0,
                            tag="meta")

    y_prompt, (conv_p, pool_p, ffn_p) = _trunk(x_prompt, meta_states, lambda l: l, batch, w,
                                               nb=batch, tt=PROMPT_TT, ffn_tt=PROMPT_FFN_TT,
                                               pos0=N_META, tag="prompt")

    carried = (time_major(state_conv), time_major(state_pool), state_ffn)
    y_sample, (conv_s, pool_s, ffn_s) = _trunk(x_sample, carried, lambda l: l, dec_batch, w,
                                               nb=SAMPLE_NB, tt=dec_seq, ffn_tt=dec_seq,
                                               pos0=past_len,
                                               tag="sample")

    return (y_prompt, y_sample, time_major(conv_p), time_major(pool_p), ffn_p,
            time_major(conv_s), time_major(pool_s), ffn_s)
```

```python
import functools

import jax
import jax.numpy as jnp
from jax import lax
from jax.experimental import pallas as pl
from jax.experimental.pallas import tpu as pltpu

D_MODEL = 1024
CONV_W = 31
POOL_WINDOWS = (2, 4, 8, 16)
POOL_MAX = max(POOL_WINDOWS)
N_POOL_GROUPS = len(POOL_WINDOWS)
POOL_GC = D_MODEL // N_POOL_GROUPS
D_FF = 2816
FFN_CONV_W = 3
N_META = 16
EPS = 1e-6

SUBLANES = 8
LANES = 128
CONV_HALO = CONV_W - 1
POOL_HALO = POOL_MAX - 1
FFN_HALO = FFN_CONV_W - 1
COL_CHUNK = 256
CONV_TIME_BLOCK = 16
V7X_VMEM_BYTES = 64 * 1024 * 1024
VMEM_LIMIT_BYTES = V7X_VMEM_BYTES // 8 * 7
PAST_LEN = 16384

_F32 = jnp.float32
_BF16 = jnp.bfloat16


def _dot(a, b):
    return jnp.dot(a, b, preferred_element_type=_F32)


def _rms(x, g):
    ms = jnp.mean(x * x, axis=-1, keepdims=True)
    return x * lax.rsqrt(ms + EPS) * g


def _to_rows(a):
    nb, r, c = a.shape
    return jnp.swapaxes(a, 0, 1).reshape(r * nb, c)


def _from_rows(a, nb):
    rows, c = a.shape
    return jnp.swapaxes(a.reshape(rows // nb, nb, c), 0, 1)


def _mixer_kernel(x_ref, cst_ref, pst_ref, zero_ref, g1_ref, win_ref, wdw_ref, cb_ref, lng_ref, lnb_ref,
                  wco_ref, wpool_ref, psc_ref, wout_ref, nconv_all_ref, npool_all_ref,
                  xo_ref, nconv_ref, npool_ref,
                  h_ref, extc_ref, extp_ref, conv_ref, ga_ref, gb_ref,
                  *, nb, tt, nt, pos0, x_natural):
    del nconv_all_ref, npool_all_ref
    d = D_MODEL
    tm = tt * nb
    hc = CONV_HALO * nb
    hp = POOL_HALO * nb
    t_idx = pl.program_id(1)

    @pl.when(t_idx == 0)
    def _():
        extc_ref[0:hc, :] = cst_ref[...].reshape(hc, d)
        extp_ref[0:hp, :] = pst_ref[...].reshape(hp, d)

    x = _to_rows(x_ref[...]) if x_natural else x_ref[...]
    h_ref[...] = _rms(x, g1_ref[...]).astype(_BF16)

    n_chunks = d // COL_CHUNK
    side = ((extp_ref, hp, 2 * d), (ga_ref, 0, 3 * d), (gb_ref, 0, 4 * d))
    assert len(side) == n_chunks - 1
    bits_of = lambda v: lax.bitcast_convert_type(v, jnp.uint32)

    def side_dots(i):
        dst, row0, wcol = side[i]
        r = _dot(h_ref[...], win_ref[:, wcol:wcol + d])
        dst[row0:row0 + tm, :] = r
        tok = bits_of(r[:, 0:COL_CHUNK])
        for lo in range(COL_CHUNK, d, COL_CHUNK):
            tok = tok | bits_of(r[:, lo:lo + COL_CHUNK])
        return tok

    def anchored(y, tok):
        return lax.bitcast_convert_type(bits_of(y) ^ (tok & zero_ref[...]), _F32)

    def glu(c):
        lo, hi = c * COL_CHUNK, (c + 1) * COL_CHUNK
        za = _dot(h_ref[...], win_ref[:, lo:hi])
        zg = _dot(h_ref[...], win_ref[:, d + lo:d + hi])
        y = za * jax.nn.sigmoid(zg)
        if c > 0:
            y = anchored(y, side_dots(c - 1))
        extc_ref[hc:hc + tm, lo:hi] = y

    def conv(c):
        for l0 in range(c * COL_CHUNK, (c + 1) * COL_CHUNK, LANES):
            l1 = l0 + LANES
            w = [wdw_ref[k, :, l0:l1] for k in range(CONV_W)]
            bias = cb_ref[:, l0:l1]
            for r_off in range(0, nb, SUBLANES):
                for t0 in range(0, tt, CONV_TIME_BLOCK):
                    n_out = min(CONV_TIME_BLOCK, tt - t0)
                    acc = [None] * n_out
                    for j in range(n_out + CONV_W - 1):
                        r = (t0 + j) * nb + r_off
                        e = extc_ref[r:r + SUBLANES, l0:l1]
                        for s in range(n_out):
                            k = j - s
                            if 0 <= k < CONV_W:
                                term = w[k] * e
                                acc[s] = term if acc[s] is None else acc[s] + term
                    for s in range(n_out):
                        r = (t0 + s) * nb + r_off
                        conv_ref[r:r + SUBLANES, l0:l1] = acc[s] + bias

    glu(0)
    for c in range(n_chunks):
        if c + 1 < n_chunks:
            glu(c + 1)
        conv(c)

    a = conv_ref[...]
    mu = jnp.mean(a, axis=-1, keepdims=True)
    ac = a - mu
    var = jnp.mean(ac * ac, axis=-1, keepdims=True)
    a = ac * lax.rsqrt(var + EPS) * lng_ref[...] + lnb_ref[...]
    a = a * jax.nn.sigmoid(a)
    a_out = _dot(a.astype(_BF16), wco_ref[...])

    if pos0 + 1 < POOL_MAX:
        row = lax.broadcasted_iota(jnp.int32, (tm, 1), 0)
        pos = pos0 + t_idx * tt + row // nb
    p_parts = []
    for gi, win in enumerate(POOL_WINDOWS):
        l0, l1 = gi * POOL_GC, (gi + 1) * POOL_GC
        cur = extp_ref[hp:hp + tm, l0:l1]
        s = cur
        for j in range(1, win):
            s = s + extp_ref[hp - j * nb:hp - j * nb + tm, l0:l1]
        if pos0 + 1 < POOL_MAX:
            mean = s / jnp.minimum(pos + 1, win).astype(_F32)
        else:
            mean = s * (1.0 / win)
        p = (mean - cur).astype(_BF16)
        p_parts.append(_dot(p, wpool_ref[gi]) * psc_ref[:, l0:l1])
    p_out = jnp.concatenate(p_parts, axis=-1)

    m = jax.nn.sigmoid(ga_ref[...]) * a_out + jax.nn.sigmoid(gb_ref[...]) * p_out
    xo_ref[...] = x + _dot(m.astype(_BF16), wout_ref[...])

    if nt > 1:
        @pl.when(t_idx < nt - 1)
        def _():
            extc_ref[0:hc, :] = extc_ref[tm:tm + hc, :]
            extp_ref[0:hp, :] = extp_ref[tm:tm + hp, :]

    @pl.when(t_idx == nt - 1)
    def _():
        nconv_ref[...] = extc_ref[tm:tm + hc, :].reshape(CONV_HALO, nb, d)
        npool_ref[...] = extp_ref[tm:tm + hp, :].reshape(POOL_HALO, nb, d)


def _ffn_kernel(x_ref, fst_ref, g2_ref, wup_ref, fdw_ref, wdown_ref, fg_ref, nffn_all_ref,
                xo_ref, nffn_ref,
                h_ref, tail_ref, act_ref, *, nb, tt, nt, final_norm, y_natural):
    del nffn_all_ref
    f = D_FF
    tm = tt * nb
    hf = FFN_HALO * nb
    t_idx = pl.program_id(1)

    @pl.when(t_idx == 0)
    def _():
        tail_ref[...] = _to_rows(fst_ref[...])

    x = x_ref[...]
    h_ref[...] = _rms(x, g2_ref[...]).astype(_BF16)

    def conv3(up, lo, hi):
        ext = jnp.concatenate([tail_ref[:, lo:hi], up], axis=0)
        y = fdw_ref[0:1, lo:hi] * ext[0:tm]
        for k in range(1, FFN_CONV_W):
            y = y + fdw_ref[k:k + 1, lo:hi] * ext[k * nb:k * nb + tm]
        tail_ref[:, lo:hi] = ext[tm:tm + hf]
        return y

    for c in range(f // COL_CHUNK):
        lo, hi = c * COL_CHUNK, (c + 1) * COL_CHUNK
        yv = conv3(_dot(h_ref[...], wup_ref[:, lo:hi]), lo, hi)
        yg = conv3(_dot(h_ref[...], wup_ref[:, f + lo:f + hi]), f + lo, f + hi)
        act_ref[:, lo:hi] = (jax.nn.gelu(yg) * yv).astype(_BF16)
    y = x + _dot(act_ref[...], wdown_ref[...])
    if final_norm:
        y = _rms(y, fg_ref[...])
    xo_ref[...] = _from_rows(y, nb) if y_natural else y

    @pl.when(t_idx == nt - 1)
    def _():
        nffn_ref[...] = _from_rows(tail_ref[...], nb)


def _const_spec(shape):
    zeros = (0,) * len(shape)
    return pl.BlockSpec(shape, lambda g, t: zeros, pipeline_mode=pl.Buffered(1))


def _layer_spec(shape, layer):
    zeros = (0,) * (len(shape) - 1)
    return pl.BlockSpec((None,) + tuple(shape[1:]), lambda g, t: (layer,) + zeros,
                        pipeline_mode=pl.Buffered(1))


def _state_tm_spec(layer, nb, steps, cols):
    return pl.BlockSpec((None, steps, nb, cols), lambda g, t: (layer, 0, g, 0))


def _state_seq_spec(layer, nb, steps, cols):
    return pl.BlockSpec((None, nb, steps, cols), lambda g, t: (layer, g, 0, 0))


_ALIASED = pl.BlockSpec(memory_space=pl.ANY)


def _natural_spec(nb, tt, cols):
    return pl.BlockSpec((nb, tt, cols), lambda g, t: (g, t, 0))


def _rows_spec(rows, cols):
    return pl.BlockSpec((None, rows, cols), lambda g, t: (g, t, 0))


def _compiler_params():
    return pltpu.CompilerParams(dimension_semantics=("arbitrary", "arbitrary"),
                                vmem_limit_bytes=VMEM_LIMIT_BYTES)


def _mixer_call(x, cst, pst, st_layer, nconv_all, npool_all, layer, w, *, nb, tt, pos0,
                x_natural, name):
    d = D_MODEL
    seqs = cst.shape[2]
    groups = seqs // nb
    steps = x.shape[1] if x_natural else x.shape[1] // nb
    tm = tt * nb
    nt = steps // tt
    assert seqs == groups * nb and steps == nt * tt
    hc, hp = CONV_HALO * nb, POOL_HALO * nb
    assert nt == 1 or tm >= hc
    kern = functools.partial(_mixer_kernel, nb=nb, tt=tt, nt=nt, pos0=pos0, x_natural=x_natural)
    weights = (w["norm1_g"], w["w_in"], w["conv_dw"], w["conv_b"], w["ln_g"], w["ln_b"],
               w["w_conv_out"], w["w_pool"], w["pool_scale"], w["w_out"])
    x_spec = _natural_spec(nb, tt, d) if x_natural else _rows_spec(tm, d)
    zero_bits = jnp.zeros((tm, COL_CHUNK), jnp.uint32)
    n_in = 4 + len(weights)
    return pl.pallas_call(
        kern,
        grid=(groups, nt),
        in_specs=[x_spec, _state_tm_spec(st_layer, nb, CONV_HALO, d),
                  _state_tm_spec(st_layer, nb, POOL_HALO, d), _const_spec(zero_bits.shape)]
        + [_layer_spec(a.shape, layer) for a in weights] + [_ALIASED, _ALIASED],
        out_specs=[_rows_spec(tm, d), _state_tm_spec(layer, nb, CONV_HALO, d),
                   _state_tm_spec(layer, nb, POOL_HALO, d)],
        out_shape=[jax.ShapeDtypeStruct((groups, steps * nb, d), _F32),
                   jax.ShapeDtypeStruct(nconv_all.shape, _F32),
                   jax.ShapeDtypeStruct(npool_all.shape, _F32)],
        input_output_aliases={n_in: 1, n_in + 1: 2},
        scratch_shapes=[pltpu.VMEM((tm, d), _BF16),
                        pltpu.VMEM((hc + tm, d), _F32),
                        pltpu.VMEM((hp + tm, d), _F32),
                        pltpu.VMEM((tm, d), _F32),
                        pltpu.VMEM((tm, d), _F32),
                        pltpu.VMEM((tm, d), _F32)],
        compiler_params=_compiler_params(),
        name=name,
    )(x, cst, pst, zero_bits, *weights, nconv_all, npool_all)


def _ffn_call(x, fst, st_layer, nffn_all, layer, w, *, nb, tt, final_norm, y_natural, name):
    d = D_MODEL
    groups, rows, _ = x.shape
    seqs = groups * nb
    steps = rows // nb
    tm = tt * nb
    nt = steps // tt
    assert steps == nt * tt
    hf = FFN_HALO * nb
    assert nt == 1 or tm >= hf
    kern = functools.partial(_ffn_kernel, nb=nb, tt=tt, nt=nt, final_norm=final_norm,
                             y_natural=y_natural)
    weights = (w["norm2_g"], w["w_up"], w["ffn_dw"], w["w_down"])
    if y_natural:
        y_spec, y_shape = _natural_spec(nb, tt, d), (seqs, steps, d)
    else:
        y_spec, y_shape = _rows_spec(tm, d), (groups, rows, d)
    n_in = 2 + len(weights) + 1
    return pl.pallas_call(
        kern,
        grid=(groups, nt),
        in_specs=[_rows_spec(tm, d), _state_seq_spec(st_layer, nb, FFN_HALO, 2 * D_FF)]
        + [_layer_spec(a.shape, layer) for a in weights]
        + [_const_spec(w["final_g"].shape), _ALIASED],
        out_specs=[y_spec, _state_seq_spec(layer, nb, FFN_HALO, 2 * D_FF)],
        out_shape=[jax.ShapeDtypeStruct(y_shape, _F32),
                   jax.ShapeDtypeStruct(nffn_all.shape, _F32)],
        input_output_aliases={n_in: 1},
        scratch_shapes=[pltpu.VMEM((tm, d), _BF16),
                        pltpu.VMEM((hf, 2 * D_FF), _F32),
                        pltpu.VMEM((tm, D_FF), _BF16)],
        compiler_params=_compiler_params(),
        name=name,
    )(x, fst, *weights, w["final_g"], nffn_all)


def _trunk(x, states, st_layer, seqs, w, *, nb, tt, ffn_tt, pos0, tag):
    depth = w["w_in"].shape[0]
    cst, pst, fst = states
    d = D_MODEL
    nconv = pl.empty((depth, CONV_HALO, seqs, d), _F32)
    npool = pl.empty((depth, POOL_HALO, seqs, d), _F32)
    nffn = pl.empty((depth, seqs, FFN_HALO, 2 * D_FF), _F32)
    for l in range(depth):
        x, nconv, npool = _mixer_call(x, cst, pst, st_layer(l), nconv, npool, l, w, nb=nb, tt=tt,
                                      pos0=pos0, x_natural=(l == 0), name=f"mixer_{tag}_{l}")
        x, nffn = _ffn_call(x, fst, st_layer(l), nffn, l, w, nb=nb, tt=ffn_tt,
                            final_norm=(l == depth - 1), y_natural=(l == depth - 1),
                            name=f"ffn_{tag}_{l}")
    return x, (nconv, npool, nffn)


PROMPT_TT = 64
PROMPT_FFN_TT = 128
SAMPLE_NB = 32


def kernel(x_prompt, x_sample, state_conv, state_pool, state_ffn, meta_tokens, norm1_g, w_in, conv_dw, conv_b, ln_g, ln_b, w_conv_out, w_pool, pool_scale, w_out, norm2_g, w_up, ffn_dw, w_down, final_g):
    depth = w_in.shape[0]
    batch, seq, d = x_prompt.shape
    dec_batch, dec_seq, _ = x_sample.shape

    row = lambda a: a[:, None, :]
    w = dict(
        norm1_g=row(norm1_g), w_in=w_in.astype(_BF16),
        conv_dw=jnp.broadcast_to(conv_dw[:, :, None, :], (depth, CONV_W, SUBLANES, d)),
        conv_b=row(conv_b), ln_g=row(ln_g), ln_b=row(ln_b),
        w_conv_out=w_conv_out.astype(_BF16), w_pool=w_pool.astype(_BF16),
        pool_scale=row(pool_scale), w_out=w_out.astype(_BF16),
        norm2_g=row(norm2_g), w_up=w_up.astype(_BF16), ffn_dw=ffn_dw,
        w_down=w_down.astype(_BF16), final_g=final_g[None])
    time_major = lambda a: jnp.transpose(a, (0, 2, 1, 3))

    xm = jnp.broadcast_to(meta_tokens[None], (batch, N_META, d))
    empty = (jnp.zeros((1, CONV_HALO, batch, d), _F32), jnp.zeros((1, POOL_HALO, batch, d), _F32),
             jnp.zeros((1, batch, FFN_HALO, 2 * D_FF), _F32))
    _, meta_states = _trunk(xm, empty, lambda l: 0, batch, w, nb=batch, tt=N_META,
                            ffn_tt=N_META, pos0=0,
                            tag="meta")

    y_prompt, (conv_p, pool_p, ffn_p) = _trunk(x_prompt, meta_states, lambda l: l, batch, w,
                                               nb=batch, tt=PROMPT_TT, ffn_tt=PROMPT_FFN_TT,
                                               pos0=N_META, tag="prompt")

    carried = (time_major(state_conv), time_major(state_pool), state_ffn)
    y_sample, (conv_s, pool_s, ffn_s) = _trunk(x_sample, carried, lambda l: l, dec_batch, w,
                                               nb=SAMPLE_NB, tt=dec_seq, ffn_tt=dec_seq,
                                               pos0=PAST_LEN,
                                               tag="sample")

    return (y_prompt, y_sample, time_major(conv_p), time_major(pool_p), ffn_p,
            time_major(conv_s), time_major(pool_s), ffn_s)
```

```python
import functools

import jax
import jax.numpy as jnp
from jax import lax
from jax.experimental import pallas as pl
from jax.experimental.pallas import tpu as pltpu

D_MODEL = 1024
CONV_W = 31
POOL_WINDOWS = (2, 4, 8, 16)
POOL_MAX = max(POOL_WINDOWS)
N_POOL_GROUPS = len(POOL_WINDOWS)
POOL_GC = D_MODEL // N_POOL_GROUPS
D_FF = 2816
FFN_CONV_W = 3
N_META = 16
EPS = 1e-6

SUBLANES = 8
LANES = 128
CONV_HALO = CONV_W - 1
POOL_HALO = POOL_MAX - 1
FFN_HALO = FFN_CONV_W - 1
COL_CHUNK = 256
CONV_TIME_BLOCK = 16
V7X_VMEM_BYTES = 64 * 1024 * 1024
VMEM_LIMIT_BYTES = V7X_VMEM_BYTES // 8 * 7
PAST_LEN = 16384

_F32 = jnp.float32
_BF16 = jnp.bfloat16


def _dot(a, b):
    return jnp.dot(a, b, preferred_element_type=_F32)


def _rms(x, g):
    ms = jnp.mean(x * x, axis=-1, keepdims=True)
    return x * lax.rsqrt(ms + EPS) * g


def _to_rows(a):
    nb, r, c = a.shape
    return jnp.swapaxes(a, 0, 1).reshape(r * nb, c)


def _from_rows(a, nb):
    rows, c = a.shape
    return jnp.swapaxes(a.reshape(rows // nb, nb, c), 0, 1)


def _mixer_kernel(x_ref, cst_ref, pst_ref, zero_ref, g1_ref, win_ref, wdw_ref, cb_ref, lng_ref, lnb_ref,
                  wco_ref, wpool_ref, psc_ref, wout_ref, nconv_all_ref, npool_all_ref,
                  xo_ref, nconv_ref, npool_ref,
                  h_ref, extc_ref, extp_ref, conv_ref, ga_ref, gb_ref,
                  *, nb, tt, nt, pos0, x_natural):
    del nconv_all_ref, npool_all_ref
    d = D_MODEL
    tm = tt * nb
    hc = CONV_HALO * nb
    hp = POOL_HALO * nb
    t_idx = pl.program_id(1)

    @pl.when(t_idx == 0)
    def _():
        extc_ref[0:hc, :] = cst_ref[...].reshape(hc, d)
        extp_ref[0:hp, :] = pst_ref[...].reshape(hp, d)

    x = _to_rows(x_ref[...]) if x_natural else x_ref[...]
    h_ref[...] = _rms(x, g1_ref[...]).astype(_BF16)

    n_chunks = d // COL_CHUNK
    side = ((extp_ref, hp, 2 * d), (ga_ref, 0, 3 * d), (gb_ref, 0, 4 * d))
    assert len(side) == n_chunks - 1
    bits_of = lambda v: lax.bitcast_convert_type(v, jnp.uint32)

    def side_dots(i):
        dst, row0, wcol = side[i]
        r = _dot(h_ref[...], win_ref[:, wcol:wcol + d])
        dst[row0:row0 + tm, :] = r
        tok = bits_of(r[:, 0:COL_CHUNK])
        for lo in range(COL_CHUNK, d, COL_CHUNK):
            tok = tok | bits_of(r[:, lo:lo + COL_CHUNK])
        return tok

    def anchored(y, tok):
        return lax.bitcast_convert_type(bits_of(y) ^ (tok & zero_ref[...]), _F32)

    pool_parts = []

    def pool_branch():
        if pos0 + 1 < POOL_MAX:
            row = lax.broadcasted_iota(jnp.int32, (tm, 1), 0)
            pos = pos0 + t_idx * tt + row // nb
        for gi, win in enumerate(POOL_WINDOWS):
            l0, l1 = gi * POOL_GC, (gi + 1) * POOL_GC
            cur = extp_ref[hp:hp + tm, l0:l1]
            s = cur
            for j in range(1, win):
                s = s + extp_ref[hp - j * nb:hp - j * nb + tm, l0:l1]
            if pos0 + 1 < POOL_MAX:
                mean = s / jnp.minimum(pos + 1, win).astype(_F32)
            else:
                mean = s * (1.0 / win)
            p = (mean - cur).astype(_BF16)
            pool_parts.append(_dot(p, wpool_ref[gi]) * psc_ref[:, l0:l1])

    def glu(c):
        lo, hi = c * COL_CHUNK, (c + 1) * COL_CHUNK
        za = _dot(h_ref[...], win_ref[:, lo:hi])
        zg = _dot(h_ref[...], win_ref[:, d + lo:d + hi])
        y = za * jax.nn.sigmoid(zg)
        if c > 0:
            tok = side_dots(c - 1)
            if c == 2:
                pool_branch()
                for part in pool_parts:
                    tok = tok | bits_of(part)
            y = anchored(y, tok)
        extc_ref[hc:hc + tm, lo:hi] = y

    def conv(c):
        for l0 in range(c * COL_CHUNK, (c + 1) * COL_CHUNK, LANES):
            l1 = l0 + LANES
            w = [wdw_ref[k, :, l0:l1] for k in range(CONV_W)]
            bias = cb_ref[:, l0:l1]
            for r_off in range(0, nb, SUBLANES):
                for t0 in range(0, tt, CONV_TIME_BLOCK):
                    n_out = min(CONV_TIME_BLOCK, tt - t0)
                    acc = [None] * n_out
                    for j in range(n_out + CONV_W - 1):
                        r = (t0 + j) * nb + r_off
                        e = extc_ref[r:r + SUBLANES, l0:l1]
                        for s in range(n_out):
                            k = j - s
                            if 0 <= k < CONV_W:
                                term = w[k] * e
                                acc[s] = term if acc[s] is None else acc[s] + term
                    for s in range(n_out):
                        r = (t0 + s) * nb + r_off
                        conv_ref[r:r + SUBLANES, l0:l1] = acc[s] + bias

    glu(0)
    for c in range(n_chunks):
        if c + 1 < n_chunks:
            glu(c + 1)
        conv(c)

    a = conv_ref[...]
    mu = jnp.mean(a, axis=-1, keepdims=True)
    ac = a - mu
    var = jnp.mean(ac * ac, axis=-1, keepdims=True)
    a = ac * lax.rsqrt(var + EPS) * lng_ref[...] + lnb_ref[...]
    a = a * jax.nn.sigmoid(a)
    a_out = _dot(a.astype(_BF16), wco_ref[...])

    p_out = jnp.concatenate(pool_parts, axis=-1)

    m = jax.nn.sigmoid(ga_ref[...]) * a_out + jax.nn.sigmoid(gb_ref[...]) * p_out
    xo_ref[...] = x + _dot(m.astype(_BF16), wout_ref[...])

    if nt > 1:
        @pl.when(t_idx < nt - 1)
        def _():
            extc_ref[0:hc, :] = extc_ref[tm:tm + hc, :]
            extp_ref[0:hp, :] = extp_ref[tm:tm + hp, :]

    @pl.when(t_idx == nt - 1)
    def _():
        nconv_ref[...] = extc_ref[tm:tm + hc, :].reshape(CONV_HALO, nb, d)
        npool_ref[...] = extp_ref[tm:tm + hp, :].reshape(POOL_HALO, nb, d)


def _ffn_kernel(x_ref, fst_ref, g2_ref, wup_ref, fdw_ref, wdown_ref, fg_ref, nffn_all_ref,
                xo_ref, nffn_ref,
                h_ref, tail_ref, act_ref, *, nb, tt, nt, final_norm, y_natural):
    del nffn_all_ref
    f = D_FF
    tm = tt * nb
    hf = FFN_HALO * nb
    t_idx = pl.program_id(1)

    @pl.when(t_idx == 0)
    def _():
        tail_ref[...] = _to_rows(fst_ref[...])

    x = x_ref[...]
    h_ref[...] = _rms(x, g2_ref[...]).astype(_BF16)

    def conv3(up, lo, hi):
        ext = jnp.concatenate([tail_ref[:, lo:hi], up], axis=0)
        y = fdw_ref[0:1, lo:hi] * ext[0:tm]
        for k in range(1, FFN_CONV_W):
            y = y + fdw_ref[k:k + 1, lo:hi] * ext[k * nb:k * nb + tm]
        tail_ref[:, lo:hi] = ext[tm:tm + hf]
        return y

    for c in range(f // COL_CHUNK):
        lo, hi = c * COL_CHUNK, (c + 1) * COL_CHUNK
        yv = conv3(_dot(h_ref[...], wup_ref[:, lo:hi]), lo, hi)
        yg = conv3(_dot(h_ref[...], wup_ref[:, f + lo:f + hi]), f + lo, f + hi)
        act_ref[:, lo:hi] = (jax.nn.gelu(yg) * yv).astype(_BF16)
    y = x + _dot(act_ref[...], wdown_ref[...])
    if final_norm:
        y = _rms(y, fg_ref[...])
    xo_ref[...] = _from_rows(y, nb) if y_natural else y

    @pl.when(t_idx == nt - 1)
    def _():
        nffn_ref[...] = _from_rows(tail_ref[...], nb)


def _const_spec(shape):
    zeros = (0,) * len(shape)
    return pl.BlockSpec(shape, lambda g, t: zeros, pipeline_mode=pl.Buffered(1))


def _layer_spec(shape, layer):
    zeros = (0,) * (len(shape) - 1)
    return pl.BlockSpec((None,) + tuple(shape[1:]), lambda g, t: (layer,) + zeros,
                        pipeline_mode=pl.Buffered(1))


def _state_tm_spec(layer, nb, steps, cols):
    return pl.BlockSpec((None, steps, nb, cols), lambda g, t: (layer, 0, g, 0))


def _state_seq_spec(layer, nb, steps, cols):
    return pl.BlockSpec((None, nb, steps, cols), lambda g, t: (layer, g, 0, 0))


_ALIASED = pl.BlockSpec(memory_space=pl.ANY)


def _natural_spec(nb, tt, cols):
    return pl.BlockSpec((nb, tt, cols), lambda g, t: (g, t, 0))


def _rows_spec(rows, cols):
    return pl.BlockSpec((None, rows, cols), lambda g, t: (g, t, 0))


def _compiler_params():
    return pltpu.CompilerParams(dimension_semantics=("arbitrary", "arbitrary"),
                                vmem_limit_bytes=VMEM_LIMIT_BYTES)


def _mixer_call(x, cst, pst, st_layer, nconv_all, npool_all, layer, w, *, nb, tt, pos0,
                x_natural, name):
    d = D_MODEL
    seqs = cst.shape[2]
    groups = seqs // nb
    steps = x.shape[1] if x_natural else x.shape[1] // nb
    tm = tt * nb
    nt = steps // tt
    assert seqs == groups * nb and steps == nt * tt
    hc, hp = CONV_HALO * nb, POOL_HALO * nb
    assert nt == 1 or tm >= hc
    kern = functools.partial(_mixer_kernel, nb=nb, tt=tt, nt=nt, pos0=pos0, x_natural=x_natural)
    weights = (w["norm1_g"], w["w_in"], w["conv_dw"], w["conv_b"], w["ln_g"], w["ln_b"],
               w["w_conv_out"], w["w_pool"], w["pool_scale"], w["w_out"])
    x_spec = _natural_spec(nb, tt, d) if x_natural else _rows_spec(tm, d)
    zero_bits = jnp.zeros((tm, COL_CHUNK), jnp.uint32)
    n_in = 4 + len(weights)
    return pl.pallas_call(
        kern,
        grid=(groups, nt),
        in_specs=[x_spec, _state_tm_spec(st_layer, nb, CONV_HALO, d),
                  _state_tm_spec(st_layer, nb, POOL_HALO, d), _const_spec(zero_bits.shape)]
        + [_layer_spec(a.shape, layer) for a in weights] + [_ALIASED, _ALIASED],
        out_specs=[_rows_spec(tm, d), _state_tm_spec(layer, nb, CONV_HALO, d),
                   _state_tm_spec(layer, nb, POOL_HALO, d)],
        out_shape=[jax.ShapeDtypeStruct((groups, steps * nb, d), _F32),
                   jax.ShapeDtypeStruct(nconv_all.shape, _F32),
                   jax.ShapeDtypeStruct(npool_all.shape, _F32)],
        input_output_aliases={n_in: 1, n_in + 1: 2},
        scratch_shapes=[pltpu.VMEM((tm, d), _BF16),
                        pltpu.VMEM((hc + tm, d), _F32),
                        pltpu.VMEM((hp + tm, d), _F32),
                        pltpu.VMEM((tm, d), _F32),
                        pltpu.VMEM((tm, d), _F32),
                        pltpu.VMEM((tm, d), _F32)],
        compiler_params=_compiler_params(),
        name=name,
    )(x, cst, pst, zero_bits, *weights, nconv_all, npool_all)


def _ffn_call(x, fst, st_layer, nffn_all, layer, w, *, nb, tt, final_norm, y_natural, name):
    d = D_MODEL
    groups, rows, _ = x.shape
    seqs = groups * nb
    steps = rows // nb
    tm = tt * nb
    nt = steps // tt
    assert steps == nt * tt
    hf = FFN_HALO * nb
    assert nt == 1 or tm >= hf
    kern = functools.partial(_ffn_kernel, nb=nb, tt=tt, nt=nt, final_norm=final_norm,
                             y_natural=y_natural)
    weights = (w["norm2_g"], w["w_up"], w["ffn_dw"], w["w_down"])
    if y_natural:
        y_spec, y_shape = _natural_spec(nb, tt, d), (seqs, steps, d)
    else:
        y_spec, y_shape = _rows_spec(tm, d), (groups, rows, d)
    n_in = 2 + len(weights) + 1
    return pl.pallas_call(
        kern,
        grid=(groups, nt),
        in_specs=[_rows_spec(tm, d), _state_seq_spec(st_layer, nb, FFN_HALO, 2 * D_FF)]
        + [_layer_spec(a.shape, layer) for a in weights]
        + [_const_spec(w["final_g"].shape), _ALIASED],
        out_specs=[y_spec, _state_seq_spec(layer, nb, FFN_HALO, 2 * D_FF)],
        out_shape=[jax.ShapeDtypeStruct(y_shape, _F32),
                   jax.ShapeDtypeStruct(nffn_all.shape, _F32)],
        input_output_aliases={n_in: 1},
        scratch_shapes=[pltpu.VMEM((tm, d), _BF16),
                        pltpu.VMEM((hf, 2 * D_FF), _F32),
                        pltpu.VMEM((tm, D_FF), _BF16)],
        compiler_params=_compiler_params(),
        name=name,
    )(x, fst, *weights, w["final_g"], nffn_all)


def _trunk(x, states, st_layer, seqs, w, *, nb, tt, ffn_tt, pos0, tag):
    depth = w["w_in"].shape[0]
    cst, pst, fst = states
    d = D_MODEL
    nconv = pl.empty((depth, CONV_HALO, seqs, d), _F32)
    npool = pl.empty((depth, POOL_HALO, seqs, d), _F32)
    nffn = pl.empty((depth, seqs, FFN_HALO, 2 * D_FF), _F32)
    for l in range(depth):
        x, nconv, npool = _mixer_call(x, cst, pst, st_layer(l), nconv, npool, l, w, nb=nb, tt=tt,
                                      pos0=pos0, x_natural=(l == 0), name=f"mixer_{tag}_{l}")
        x, nffn = _ffn_call(x, fst, st_layer(l), nffn, l, w, nb=nb, tt=ffn_tt,
                            final_norm=(l == depth - 1), y_natural=(l == depth - 1),
                            name=f"ffn_{tag}_{l}")
    return x, (nconv, npool, nffn)


PROMPT_TT = 64
PROMPT_FFN_TT = 128
SAMPLE_NB = 32


def kernel(x_prompt, x_sample, state_conv, state_pool, state_ffn, meta_tokens, norm1_g, w_in, conv_dw, conv_b, ln_g, ln_b, w_conv_out, w_pool, pool_scale, w_out, norm2_g, w_up, ffn_dw, w_down, final_g):
    depth = w_in.shape[0]
    batch, seq, d = x_prompt.shape
    dec_batch, dec_seq, _ = x_sample.shape

    row = lambda a: a[:, None, :]
    w = dict(
        norm1_g=row(norm1_g), w_in=w_in.astype(_BF16),
        conv_dw=jnp.broadcast_to(conv_dw[:, :, None, :], (depth, CONV_W, SUBLANES, d)),
        conv_b=row(conv_b), ln_g=row(ln_g), ln_b=row(ln_b),
        w_conv_out=w_conv_out.astype(_BF16), w_pool=w_pool.astype(_BF16),
        pool_scale=row(pool_scale), w_out=w_out.astype(_BF16),
        norm2_g=row(norm2_g), w_up=w_up.astype(_BF16), ffn_dw=ffn_dw,
        w_down=w_down.astype(_BF16), final_g=final_g[None])
    time_major = lambda a: jnp.transpose(a, (0, 2, 1, 3))

    xm = jnp.broadcast_to(meta_tokens[None], (batch, N_META, d))
    empty = (jnp.zeros((1, CONV_HALO, batch, d), _F32), jnp.zeros((1, POOL_HALO, batch, d), _F32),
             jnp.zeros((1, batch, FFN_HALO, 2 * D_FF), _F32))
    _, meta_states = _trunk(xm, empty, lambda l: 0, batch, w, nb=batch, tt=N_META,
                            ffn_tt=N_META, pos0=0,
                            tag="meta")

    y_prompt, (conv_p, pool_p, ffn_p) = _trunk(x_prompt, meta_states, lambda l: l, batch, w,
                                               nb=batch, tt=PROMPT_TT, ffn_tt=PROMPT_FFN_TT,
                                               pos0=N_META, tag="prompt")

    carried = (time_major(state_conv), time_major(state_pool), state_ffn)
    y_sample, (conv_s, pool_s, ffn_s) = _trunk(x_sample, carried, lambda l: l, dec_batch, w,
                                               nb=SAMPLE_NB, tt=dec_seq, ffn_tt=dec_seq,
                                               pos0=PAST_LEN,
                                               tag="sample")

    return (y_prompt, y_sample, time_major(conv_p), time_major(pool_p), ffn_p,
            time_major(conv_s), time_major(pool_s), ffn_s)
```
